```python
import math
import jax, jax.numpy as jnp
from jax import lax
import numpy as np

D_MODEL = 4096
BATCH = 4
SEQ = 2048
DEPTH = 1
DEC_BATCH = 128
DEC_SEQ = 8
PAST_LEN = 8192
PAGE_SIZE = 128

RWKV_HEADS = 32
RWKV_HEAD_SIZE = 64
RWKV_W = RWKV_HEADS * RWKV_HEAD_SIZE
DECAY_LORA = 128
AAA_LORA = 128
SHIFT_W = 3 * RWKV_W + DECAY_LORA + AAA_LORA
GN_EPS = 64e-5
N_Q_HEADS = 32
N_KV_HEADS = 4
HEAD_DIM = 64
GROUP = N_Q_HEADS // N_KV_HEADS
ATT_W = N_Q_HEADS * HEAD_DIM
KV_W = N_KV_HEADS * HEAD_DIM
WINDOW = 128
ROT_DIM = HEAD_DIM // 4
ROPE_THETA = 500000.0
_OFF_SHIFT = SHIFT_W
_OFF_GR = _OFF_SHIFT + RWKV_W
_OFF_Q = _OFF_GR + ATT_W
_OFF_K = _OFF_Q + KV_W
_OFF_V = _OFF_K + KV_W
_OFF_GA = _OFF_V + ATT_W
IN_SPLITS = (_OFF_SHIFT, _OFF_GR, _OFF_Q, _OFF_K, _OFF_V, _OFF_GA)
N_IN = _OFF_GA + 2 * D_MODEL
ALPHA = (2.0 * DEPTH) ** 0.25
BETA = (8.0 * DEPTH) ** -0.25
LN_EPS = 1e-5

kernel_name = "rwkv7_swa_sink_gated_hybrid_step"


def _layernorm(h, g, b):
    hf = h.astype(jnp.float32)
    m = hf.mean(-1, keepdims=True)
    var = jnp.square(hf - m).mean(-1, keepdims=True)
    return ((hf - m) * lax.rsqrt(var + LN_EPS) * g.astype(jnp.float32) + b.astype(jnp.float32)).astype(h.dtype)


def _rope_partial(x, pos):
    half = ROT_DIM // 2
    inv = ROPE_THETA ** (-jnp.arange(half, dtype=jnp.float32) * 2.0 / ROT_DIM)
    ang = pos.astype(jnp.float32)[:, None] * inv[None, :]
    cos = jnp.cos(ang)[:, None, :].astype(x.dtype)
    sin = jnp.sin(ang)[:, None, :].astype(x.dtype)
    x1, x2, rest = x[..., :half], x[..., half:ROT_DIM], x[..., ROT_DIM:]
    return jnp.concatenate([x1 * cos - x2 * sin, x2 * cos + x1 * sin, rest], axis=-1)


def _sink_attention(q, k, v, mask, sinks):
    qg = q.reshape(q.shape[:-2] + (N_KV_HEADS, GROUP, HEAD_DIM))
    s = jnp.einsum("...qhgd,...shd->...hgqs", qg, k).astype(jnp.float32) / math.sqrt(HEAD_DIM)
    s = jnp.where(mask[..., None, None, :, :], s, -1e30)
    sink = jnp.broadcast_to(sinks.astype(jnp.float32).reshape(N_KV_HEADS, GROUP, 1, 1), s.shape[:-1] + (1,))
    p = jax.nn.softmax(jnp.concatenate([s, sink], axis=-1), axis=-1)[..., :-1].astype(v.dtype)
    o = jnp.einsum("...hgqs,...shd->...qhgd", p, v)
    return o.reshape(o.shape[:-3] + (ATT_W,))


def _window_attention_prompt(q, k, v, sinks):
    B, T = q.shape[0], q.shape[1]
    nb = T // WINDOW
    qb = q.reshape(B, nb, WINDOW, N_Q_HEADS, HEAD_DIM)
    kb = k.reshape(B, nb, WINDOW, N_KV_HEADS, HEAD_DIM)
    vb = v.reshape(B, nb, WINDOW, N_KV_HEADS, HEAD_DIM)
    prev = lambda t: jnp.concatenate([jnp.zeros_like(t[:, :1]), t[:, :-1]], axis=1)
    kcat = jnp.concatenate([prev(kb), kb], axis=2)
    vcat = jnp.concatenate([prev(vb), vb], axis=2)
    qi = jnp.arange(WINDOW)[:, None]
    kj = jnp.arange(2 * WINDOW)[None, :]
    rel = WINDOW + qi - kj
    band = (rel >= 0) & (rel < WINDOW)
    valid = (jnp.arange(nb)[:, None, None] > 0) | (kj >= WINDOW)[None]
    mask = band[None] & valid
    o = _sink_attention(qb, kcat, vcat, mask, sinks)
    return o.reshape(B, T, ATT_W)


def _window_attention_sample(q, k_new, v_new, win_k, win_v, sinks):
    T = q.shape[1]
    kcat = jnp.concatenate([win_k.astype(k_new.dtype), k_new], axis=1)
    vcat = jnp.concatenate([win_v.astype(v_new.dtype), v_new], axis=1)
    qpos = jnp.arange(T) + WINDOW
    kpos = jnp.arange(WINDOW + T)
    rel = qpos[:, None] - kpos[None, :]
    mask = (rel >= 0) & (rel < WINDOW)
    o = _sink_attention(q, kcat, vcat, mask, sinks)
    return o, kcat[:, -WINDOW:], vcat[:, -WINDOW:]


def _rwkv7_branch(zs, g, shift_prev, S0, mu, w0, w_up, a0, a_up, k_k, k_a, r_k, gn_w, gn_b):
    B, T = zs.shape[0], zs.shape[1]
    prev = jnp.concatenate([shift_prev[:, None].astype(zs.dtype), zs[:, :-1]], axis=1)
    xs = zs + mu * (prev - zs)
    r, k, v, wd, ad = jnp.split(xs, [RWKV_W, 2 * RWKV_W, 3 * RWKV_W, 3 * RWKV_W + DECAY_LORA], axis=-1)
    wlog = -jax.nn.softplus(-(w0 + jnp.tanh(wd) @ w_up).astype(jnp.float32)) - 0.5
    decay = jnp.exp(-jnp.exp(wlog))
    a = jax.nn.sigmoid((a0 + ad @ a_up).astype(jnp.float32))
    heads = lambda t: t.astype(jnp.float32).reshape(B, T, RWKV_HEADS, RWKV_HEAD_SIZE)
    r_h, k_h, v_h, dec_h, a_h = heads(r), heads(k), heads(v), heads(decay), heads(a)
    kk = heads(k * k_k)
    kk = kk / jnp.maximum(jnp.sqrt(jnp.sum(kk * kk, axis=-1, keepdims=True)), 1e-12)
    k_h = k_h * (1.0 + (a_h - 1.0) * k_a.astype(jnp.float32).reshape(RWKV_HEADS, RWKV_HEAD_SIZE))

    def step(S, inp):
        r_t, k_t, v_t, w_t, kk_t, a_t = inp
        sa = jnp.einsum("bhvk,bhk->bhv", S, -kk_t)
        S = S * w_t[:, :, None, :] + sa[..., None] * (kk_t * a_t)[:, :, None, :] + v_t[..., None] * k_t[:, :, None, :]
        return S, jnp.einsum("bhvk,bhk->bhv", S, r_t)

    tm = lambda t: jnp.swapaxes(t, 0, 1)
    S_fin, ys = lax.scan(step, S0.astype(jnp.float32), (tm(r_h), tm(k_h), tm(v_h), tm(dec_h), tm(kk), tm(a_h)))
    y = tm(ys)
    m = y.mean(-1, keepdims=True)
    var = jnp.square(y - m).mean(-1, keepdims=True)
    y = ((y - m) * lax.rsqrt(var + GN_EPS)).reshape(B, T, RWKV_W) * gn_w.astype(jnp.float32) + gn_b.astype(jnp.float32)
    bonus = (jnp.sum(r_h * k_h * r_k.astype(jnp.float32), axis=-1, keepdims=True) * v_h).reshape(B, T, RWKV_W)
    out = (y + bonus).astype(zs.dtype) * jax.nn.silu(g)
    return out, S_fin.astype(S0.dtype), zs[:, -1]


def _hybrid_layer(x, pos, S0, shift0, win_k, win_v, p):
    (w_in, mu_shift, w0, w_decay_up, a0, w_aaa_up, k_k, k_a, r_k, gn_w, gn_b,
     sinks, w_branch_rwkv, w_branch_attn, w_out, ln_g, ln_b) = p
    B, T = x.shape[0], x.shape[1]
    z = jnp.einsum("btd,dn->btn", x, w_in)
    zs, g_r, q, k, v, g_a, gates = jnp.split(z, IN_SPLITS, axis=-1)
    y_r, S_new, shift_new = _rwkv7_branch(zs, g_r, shift0, S0, mu_shift, w0, w_decay_up, a0, w_aaa_up,
                                          k_k, k_a, r_k, gn_w, gn_b)
    q = _rope_partial(q.reshape(B, T, N_Q_HEADS, HEAD_DIM), pos)
    k = _rope_partial(k.reshape(B, T, N_KV_HEADS, HEAD_DIM), pos)
    v = v.reshape(B, T, N_KV_HEADS, HEAD_DIM)
    if win_k is None:
        o = _window_attention_prompt(q, k, v, sinks)
        k_buf, v_buf = k[:, -WINDOW:], v[:, -WINDOW:]
    else:
        o, k_buf, v_buf = _window_attention_sample(q, k, v, win_k, win_v, sinks)
    y_a = o * jax.nn.silu(g_a)
    m_r, m_a = jnp.split(gates, 2, axis=-1)
    mix = jax.nn.sigmoid(m_r) * (y_r @ w_branch_rwkv) + jax.nn.sigmoid(m_a) * (y_a @ w_branch_attn)
    out = mix @ w_out
    y = _layernorm(ALPHA * x + out, ln_g, ln_b)
    return y, S_new, shift_new, k_buf, v_buf


def setup_inputs(seed: int = 0) -> dict:
    key = jax.random.key(seed)
    ks = jax.random.split(key, 24)
    nrm = jax.random.normal
    L = DEPTH
    return {
        "x_prompt": nrm(ks[0], (BATCH, SEQ, D_MODEL), jnp.float32),
        "x_sample": nrm(ks[1], (DEC_BATCH, DEC_SEQ, D_MODEL), jnp.float32),
        "state_wkv": 0.2 * nrm(ks[2], (L, DEC_BATCH, RWKV_HEADS, RWKV_HEAD_SIZE, RWKV_HEAD_SIZE), jnp.float32),
        "state_shift": nrm(ks[3], (L, DEC_BATCH, SHIFT_W), jnp.float32),
        "cache_win_k": nrm(ks[4], (L, DEC_BATCH, WINDOW, N_KV_HEADS, HEAD_DIM), jnp.float32),
        "cache_win_v": nrm(ks[5], (L, DEC_BATCH, WINDOW, N_KV_HEADS, HEAD_DIM), jnp.float32),
        "w_in": nrm(ks[6], (L, D_MODEL, N_IN), jnp.float32) * D_MODEL ** -0.5,
        "mu_shift": jax.random.uniform(ks[7], (L, SHIFT_W), jnp.float32),
        "w0": jax.random.uniform(ks[8], (L, RWKV_W), jnp.float32, minval=-4.0, maxval=1.0),
        "w_decay_up": 0.5 * nrm(ks[9], (L, DECAY_LORA, RWKV_W), jnp.float32) * DECAY_LORA ** -0.5,
        "a0": 0.1 * nrm(ks[10], (L, RWKV_W), jnp.float32),
        "w_aaa_up": 0.5 * nrm(ks[11], (L, AAA_LORA, RWKV_W), jnp.float32) * AAA_LORA ** -0.5,
        "k_k": 0.85 + 0.05 * nrm(ks[12], (L, RWKV_W), jnp.float32),
        "k_a": 1.0 + 0.05 * nrm(ks[13], (L, RWKV_W), jnp.float32),
        "r_k": 0.1 * nrm(ks[14], (L, RWKV_HEADS, RWKV_HEAD_SIZE), jnp.float32),
        "gn_w": 1.0 + 0.05 * nrm(ks[15], (L, RWKV_W), jnp.float32),
        "gn_b": 0.02 * nrm(ks[16], (L, RWKV_W), jnp.float32),
        "sinks": 0.5 * nrm(ks[17], (L, N_Q_HEADS), jnp.float32),
        "w_branch_rwkv": nrm(ks[18], (L, RWKV_W, D_MODEL), jnp.float32) * RWKV_W ** -0.5 * BETA,
        "w_branch_attn": nrm(ks[19], (L, ATT_W, D_MODEL), jnp.float32) * ATT_W ** -0.5 * BETA,
        "w_out": nrm(ks[20], (L, D_MODEL, D_MODEL), jnp.float32) * D_MODEL ** -0.5 * BETA,
        "ln_g": 1.0 + 0.05 * nrm(ks[21], (L, D_MODEL), jnp.float32),
        "ln_b": 0.02 * nrm(ks[22], (L, D_MODEL), jnp.float32),
    }


def reference(x_prompt, x_sample, state_wkv, state_shift, cache_win_k, cache_win_v,
              w_in, mu_shift, w0, w_decay_up, a0, w_aaa_up, k_k, k_a, r_k, gn_w, gn_b,
              sinks, w_branch_rwkv, w_branch_attn, w_out, ln_g, ln_b):
    Bp, Tp = x_prompt.shape[0], x_prompt.shape[1]
    Ts = x_sample.shape[1]
    past = state_wkv.shape[0] * 0 + PAST_LEN
    pos_p = jnp.arange(Tp)
    pos_s = past + jnp.arange(Ts)
    yp, ys = x_prompt, x_sample
    wkv_p, sh_p, kp_l, vp_l, wkv_s, sh_s, ks_l, vs_l = [], [], [], [], [], [], [], []
    for l in range(DEPTH):
        p = (w_in[l], mu_shift[l], w0[l], w_decay_up[l], a0[l], w_aaa_up[l], k_k[l], k_a[l], r_k[l],
             gn_w[l], gn_b[l], sinks[l], w_branch_rwkv[l], w_branch_attn[l], w_out[l], ln_g[l], ln_b[l])
        S0 = jnp.zeros((Bp, RWKV_HEADS, RWKV_HEAD_SIZE, RWKV_HEAD_SIZE), x_prompt.dtype)
        sh0 = jnp.zeros((Bp, SHIFT_W), x_prompt.dtype)
        yp, a1, a2, a3, a4 = _hybrid_layer(yp, pos_p, S0, sh0, None, None, p)
        ys, b1, b2, b3, b4 = _hybrid_layer(ys, pos_s, state_wkv[l], state_shift[l], cache_win_k[l], cache_win_v[l], p)
        wkv_p.append(a1); sh_p.append(a2); kp_l.append(a3); vp_l.append(a4)
        wkv_s.append(b1); sh_s.append(b2); ks_l.append(b3); vs_l.append(b4)
    new_wkv_p = jnp.stack(wkv_p)
    new_shift_p = jnp.stack(sh_p)
    new_k_p = jnp.stack(kp_l)
    new_v_p = jnp.stack(vp_l)
    new_wkv_s = jnp.stack(wkv_s)
    new_shift_s = jnp.stack(sh_s)
    new_k_s = jnp.stack(ks_l)
    new_v_s = jnp.stack(vs_l)
    return (yp, ys, new_wkv_p, new_shift_p, new_k_p, new_v_p, new_wkv_s, new_shift_s, new_k_s, new_v_s)
```

```python
import functools
import math

import jax
import jax.numpy as jnp
from jax import lax
from jax.experimental import pallas as pl
from jax.experimental.pallas import tpu as pltpu

RWKV_HEADS = 32
HEAD = 64
RWKV_W = RWKV_HEADS * HEAD
LORA = 128
SHIFT_W = 3 * RWKV_W + 2 * LORA
GN_EPS = 64e-5
N_Q_HEADS = 32
N_KV_HEADS = 4
ATT_W = N_Q_HEADS * HEAD
KV_W = N_KV_HEADS * HEAD
WINDOW = 128
ROT_DIM = HEAD // 4
ROPE_THETA = 500000.0
PAST_LEN = 8192
DEPTH = 1
ALPHA = (2.0 * DEPTH) ** 0.25
LN_EPS = 1e-5
LANES = 128
DECAY_SCALE = math.exp(-0.5)
VMEM_LIMIT = 56 * 1024 * 1024

BF16 = jnp.bfloat16
F32 = jnp.float32


def _params(*sem):
    return pltpu.CompilerParams(dimension_semantics=sem, vmem_limit_bytes=VMEM_LIMIT)


def _tile(n, prefs):
    for p in prefs:
        if n % p == 0:
            return p
    return n


def _sigmoid(x):
    return 1.0 / (1.0 + jnp.exp(-x))


def _mm_kernel(x_ref, w_ref, o_ref, *, act):
    acc = jnp.dot(x_ref[...], w_ref[...], preferred_element_type=F32)
    if act == "silu":
        acc = acc * _sigmoid(acc)
    elif act == "sigmoid":
        acc = _sigmoid(acc)
    o_ref[...] = acc.astype(o_ref.dtype)


def _matmul(x, w, out_dtype, act=None, name="proj"):
    M, K = x.shape
    N = w.shape[1]
    tm = _tile(M, (1024, 512, 256, 128))
    tn = _tile(N, (512, 640, 256, 128))
    return pl.pallas_call(
        functools.partial(_mm_kernel, act=act),
        out_shape=jax.ShapeDtypeStruct((M, N), out_dtype),
        grid=(M // tm, N // tn),
        in_specs=[pl.BlockSpec((tm, K), lambda i, j: (i, 0)),
                  pl.BlockSpec((K, tn), lambda i, j: (0, j))],
        out_specs=pl.BlockSpec((tm, tn), lambda i, j: (i, j)),
        compiler_params=_params("parallel", "arbitrary"),
        name=name,
    )(x, w)


def _head_sum(x, j_ref):
    hi = x.astype(BF16)
    lo = (x - hi.astype(F32)).astype(BF16)
    cols = []
    for c in range(x.shape[1] // LANES):
        sl = slice(c * LANES, (c + 1) * LANES)
        cols.append(jnp.dot(hi[:, sl], j_ref[...], preferred_element_type=F32)
                    + jnp.dot(lo[:, sl], j_ref[...], preferred_element_type=F32))
    return jnp.concatenate(cols, axis=1)


def _prep_kernel(zs_ref, sh_ref, mu_ref, w0_ref, wup_ref, a0_ref, aup_ref, kk_ref, ka_ref, rk_ref, j_ref,
                 r_o, w_o, kap_o, b_o, kt_o, v_o, bon_o, carry):
    @pl.when(pl.program_id(1) == 0)
    def _():
        carry[...] = sh_ref[0]

    zs = zs_ref[0]
    tb = zs.shape[0]
    rolled = pltpu.roll(zs, 1, axis=0)
    row = lax.broadcasted_iota(jnp.int32, (tb, 1), 0)
    prev = jnp.where(row == 0, carry[...], rolled)
    carry[...] = zs[tb - 1:tb, :]
    xs = zs + mu_ref[...] * (prev - zs)
    r = xs[:, 0:RWKV_W]
    k = xs[:, RWKV_W:2 * RWKV_W]
    v = xs[:, 2 * RWKV_W:3 * RWKV_W]
    wd = xs[:, 3 * RWKV_W:3 * RWKV_W + LORA]
    ad = xs[:, 3 * RWKV_W + LORA:SHIFT_W]
    wl = w0_ref[...] + jnp.dot(jnp.tanh(wd).astype(BF16), wup_ref[...], preferred_element_type=F32)
    dec = jnp.exp(-DECAY_SCALE * _sigmoid(wl))
    a = _sigmoid(a0_ref[...] + jnp.dot(ad.astype(BF16), aup_ref[...], preferred_element_type=F32))
    kkv = k * kk_ref[...]
    norm = jnp.maximum(jnp.sqrt(_head_sum(kkv * kkv, j_ref)), 1e-12)
    kap = kkv / norm
    kt = k * (1.0 + (a - 1.0) * ka_ref[...])
    r_o[0] = r
    w_o[0] = dec
    kap_o[0] = kap
    b_o[0] = kap * a
    kt_o[0] = kt
    v_o[0] = v
    bon_o[0] = _head_sum(r * kt * rk_ref[...], j_ref) * v


def _rwkv_prep(zs, shift0, p):
    B, T, _ = zs.shape
    tb = _tile(T, (256, 128, 64, 32, 16, 8))
    row = lambda n: pl.BlockSpec((1, n), lambda b, i: (0, 0))
    full = lambda a: pl.BlockSpec(a.shape, lambda b, i: (0, 0))
    out = jax.ShapeDtypeStruct((B, T, RWKV_W), F32)
    ospec = pl.BlockSpec((1, tb, RWKV_W), lambda b, i: (b, i, 0))
    return pl.pallas_call(
        _prep_kernel,
        out_shape=(out,) * 7,
        grid=(B, T // tb),
        in_specs=[pl.BlockSpec((1, tb, SHIFT_W), lambda b, i: (b, i, 0)),
                  pl.BlockSpec((1, 1, SHIFT_W), lambda b, i: (b, 0, 0)),
                  row(SHIFT_W), row(RWKV_W), full(p["wup"]), row(RWKV_W), full(p["aup"]),
                  row(RWKV_W), row(RWKV_W), row(RWKV_W), full(p["jmat"])],
        out_specs=(ospec,) * 7,
        scratch_shapes=[pltpu.VMEM((1, SHIFT_W), F32)],
        compiler_params=_params("parallel", "arbitrary"),
        name="rwkv_prep",
    )(zs, shift0, p["mu"], p["w0"], p["wup"], p["a0"], p["aup"], p["k_k"], p["k_a"], p["r_k"], p["jmat"])


def _scan_kernel(r_ref, w_ref, kap_ref, b_ref, kt_ref, v_ref, s0_ref, y_ref, sout_ref, S):
    ti = pl.program_id(1)

    @pl.when(ti == 0)
    def _():
        S[...] = s0_ref[...]

    tb = r_ref.shape[0]
    nacc = 4

    def step(t, c):
        acc = [None] * nacc
        for k in range(HEAD):
            term = S[k] * kap_ref[t, k:k + 1, :]
            acc[k % nacc] = term if acc[k % nacc] is None else acc[k % nacc] + term
        u = (acc[0] + acc[1]) + (acc[2] + acc[3])
        vt = v_ref[t]
        acc = [None] * nacc
        for k in range(HEAD):
            s = S[k] * w_ref[t, k:k + 1, :] - u * b_ref[t, k:k + 1, :] + vt * kt_ref[t, k:k + 1, :]
            S[k] = s
            term = s * r_ref[t, k:k + 1, :]
            acc[k % nacc] = term if acc[k % nacc] is None else acc[k % nacc] + term
        y = (acc[0] + acc[1]) + (acc[2] + acc[3])
        m = jnp.mean(y, axis=0, keepdims=True)
        d = y - m
        var = jnp.mean(d * d, axis=0, keepdims=True)
        y_ref[t] = d * lax.rsqrt(var + GN_EPS)
        return c

    lax.fori_loop(0, tb, step, 0)

    @pl.when(ti == pl.num_programs(1) - 1)
    def _():
        sout_ref[...] = S[...]


def _wkv_scan(ops, s0):
    T, _, NH = ops[0].shape
    G = NH // LANES
    tb = _tile(T, (32, 16, 8))
    ospec = pl.BlockSpec((tb, HEAD, LANES), lambda g, i: (i, 0, g))
    sspec = pl.BlockSpec((HEAD, HEAD, LANES), lambda g, i: (0, 0, g))
    return pl.pallas_call(
        _scan_kernel,
        out_shape=(jax.ShapeDtypeStruct((T, HEAD, NH), F32), jax.ShapeDtypeStruct((HEAD, HEAD, NH), F32)),
        grid=(G, T // tb),
        in_specs=[ospec] * 6 + [sspec],
        out_specs=(ospec, sspec),
        scratch_shapes=[pltpu.VMEM((HEAD, HEAD, LANES), F32)],
        compiler_params=_params("parallel", "arbitrary"),
        name="wkv_scan",
    )(*ops, s0)


def _to_head_lanes(x):
    B, T, _ = x.shape
    return x.reshape(B, T, RWKV_HEADS, HEAD).transpose(1, 3, 0, 2).reshape(T, HEAD, B * RWKV_HEADS)


def _from_head_lanes(y, B):
    T = y.shape[0]
    return y.reshape(T, HEAD, B, RWKV_HEADS).transpose(2, 0, 3, 1).reshape(B, T, RWKV_W)


def _gate_kernel(yn_ref, bon_ref, g_ref, gw_ref, gb_ref, o_ref):
    y = yn_ref[...] * gw_ref[...] + gb_ref[...] + bon_ref[...]
    o_ref[...] = (y * g_ref[...].astype(F32)).astype(o_ref.dtype)


def _rwkv_gate(yn, bonus, g, gn_w, gn_b):
    M = yn.shape[0]
    tm = _tile(M, (512, 256, 128))
    spec = pl.BlockSpec((tm, RWKV_W), lambda i: (i, 0))
    row = pl.BlockSpec((1, RWKV_W), lambda i: (0, 0))
    return pl.pallas_call(
        _gate_kernel,
        out_shape=jax.ShapeDtypeStruct((M, RWKV_W), BF16),
        grid=(M // tm,),
        in_specs=[spec, spec, spec, row, row],
        out_specs=spec,
        compiler_params=_params("parallel"),
        name="rwkv_gate",
    )(yn, bonus, g, gn_w, gn_b)


def _rope_tables(pos):
    half = ROT_DIM // 2
    inv = ROPE_THETA ** (-jnp.arange(half, dtype=F32) * 2.0 / ROT_DIM)
    ang = pos.astype(F32)[:, None] * inv[None, :]
    cos, sin = jnp.cos(ang), jnp.sin(ang)
    n = pos.shape[0]
    one = jnp.ones((n, HEAD - ROT_DIM), F32)
    zero = jnp.zeros((n, HEAD - ROT_DIM), F32)
    zh = jnp.zeros((n, half), F32)
    c = jnp.concatenate([cos, cos, one], axis=1)
    s1 = jnp.concatenate([zh, sin, zero], axis=1)
    s2 = jnp.concatenate([-sin, zh, zero], axis=1)
    reps = LANES // HEAD
    return jnp.tile(c, (1, reps)), jnp.tile(s1, (1, reps)), jnp.tile(s2, (1, reps))


def _rope(x, c, s1, s2):
    half = ROT_DIM // 2
    return x * c + pltpu.roll(x, half, axis=1) * s1 + pltpu.roll(x, LANES - half, axis=1) * s2


def _dup_heads(x):
    sw = pltpu.roll(x, HEAD, axis=1)
    low = lax.broadcasted_iota(jnp.int32, x.shape, 1) < HEAD
    return jnp.where(low, x, sw), jnp.where(low, sw, x)


def _pair_attention(q, k2, v2, mask, sink0, sink1):
    n = q.shape[0]
    low = lax.broadcasted_iota(jnp.int32, q.shape, 1) < HEAD
    qs = jnp.concatenate([jnp.where(low, q, 0.0), jnp.where(low, 0.0, q)], axis=0).astype(BF16)
    s = lax.dot_general(qs, k2, (((1,), (1,)), ((), ())), preferred_element_type=F32)
    s = s * (1.0 / math.sqrt(HEAD))
    s = jnp.where(mask, s, -1e30)
    top = lax.broadcasted_iota(jnp.int32, (2 * n, 1), 0) < n
    sink = jnp.where(top, sink0, sink1)
    m = jnp.maximum(jnp.max(s, axis=1, keepdims=True), sink)
    e = jnp.exp(s - m)
    den = jnp.sum(e, axis=1, keepdims=True) + jnp.exp(sink - m)
    p = (e / den).astype(BF16)
    o = jnp.dot(p, v2, preferred_element_type=F32)
    return jnp.where(low, o[:n], o[n:])


def _attn_prompt_kernel(sink_ref, q_ref, kc_ref, kp_ref, vc_ref, vp_ref, g_ref,
                        cc_ref, s1c_ref, s2c_ref, cp_ref, s1p_ref, s2p_ref, o_ref, krot_ref):
    i = pl.program_id(1)
    W = WINDOW
    qi = lax.broadcasted_iota(jnp.int32, (2 * W, 2 * W), 0) % W
    kj = lax.broadcasted_iota(jnp.int32, (2 * W, 2 * W), 1)
    rel = W + qi - kj
    mask = (rel >= 0) & (rel < W) & ((i > 0) | (kj >= W))
    cc, s1c, s2c = cc_ref[...], s1c_ref[...], s2c_ref[...]
    cp, s1p, s2p = cp_ref[...], s1p_ref[...], s2p_ref[...]
    k2, v2 = [], []
    for t in range(KV_W // LANES):
        sl = slice(t * LANES, (t + 1) * LANES)
        kc = _rope(kc_ref[:, sl], cc, s1c, s2c)
        krot_ref[:, sl] = kc
        kcat = jnp.concatenate([_rope(kp_ref[:, sl], cp, s1p, s2p), kc], axis=0)
        vcat = jnp.concatenate([vp_ref[:, sl], vc_ref[:, sl]], axis=0)
        k2.extend(x.astype(BF16) for x in _dup_heads(kcat))
        v2.extend(x.astype(BF16) for x in _dup_heads(vcat))
    pairs_per_kv = N_Q_HEADS // N_KV_HEADS // 2
    for p in range(N_Q_HEADS // 2):
        sl = slice(p * LANES, (p + 1) * LANES)
        q = _rope(q_ref[:, sl], cc, s1c, s2c)
        g = p // pairs_per_kv
        o = _pair_attention(q, k2[g], v2[g], mask, sink_ref[2 * p], sink_ref[2 * p + 1])
        o_ref[:, sl] = (o * g_ref[:, sl].astype(F32)).astype(o_ref.dtype)


def _attn_prompt(qkv, g_a, sinks, B, T):
    W = WINDOW
    nb = T // W
    tabs = _rope_tables(jnp.arange(T))
    kcol = ATT_W // KV_W
    cur = lambda b, i: (b * nb + i, 0)
    prv = lambda b, i: (b * nb + jnp.maximum(i - 1, 0), 0)
    tcur = pl.BlockSpec((W, LANES), lambda b, i: (i, 0))
    tprv = pl.BlockSpec((W, LANES), lambda b, i: (jnp.maximum(i - 1, 0), 0))
    return pl.pallas_call(
        _attn_prompt_kernel,
        out_shape=(jax.ShapeDtypeStruct((B * T, ATT_W), BF16), jax.ShapeDtypeStruct((B * T, KV_W), F32)),
        grid=(B, nb),
        in_specs=[pl.BlockSpec(memory_space=pltpu.SMEM),
                  pl.BlockSpec((W, ATT_W), cur),
                  pl.BlockSpec((W, KV_W), lambda b, i: (b * nb + i, kcol)),
                  pl.BlockSpec((W, KV_W), lambda b, i: (b * nb + jnp.maximum(i - 1, 0), kcol)),
                  pl.BlockSpec((W, KV_W), lambda b, i: (b * nb + i, kcol + 1)),
                  pl.BlockSpec((W, KV_W), lambda b, i: (b * nb + jnp.maximum(i - 1, 0), kcol + 1)),
                  pl.BlockSpec((W, ATT_W), cur),
                  tcur, tcur, tcur, tprv, tprv, tprv],
        out_specs=(pl.BlockSpec((W, ATT_W), cur), pl.BlockSpec((W, KV_W), cur)),
        compiler_params=_params("parallel", "parallel"),
        name="attn_prompt",
    )(sinks, qkv, qkv, qkv, qkv, qkv, g_a, *tabs, *tabs)


def _attn_sample_kernel(sink_ref, q_ref, kn_ref, vn_ref, ck_ref, cv_ref, g_ref, c_ref, s1_ref, s2_ref,
                        o_ref, ko_ref, vo_ref):
    W = WINDOW
    nbat, _, _ = ck_ref.shape
    Ts = q_ref.shape[0] // nbat
    qi = lax.broadcasted_iota(jnp.int32, (2 * Ts, W + Ts), 0) % Ts
    kj = lax.broadcasted_iota(jnp.int32, (2 * Ts, W + Ts), 1)
    rel = W + qi - kj
    mask = (rel >= 0) & (rel < W)
    c, s1, s2 = c_ref[...], s1_ref[...], s2_ref[...]
    pairs_per_kv = N_Q_HEADS // N_KV_HEADS // 2
    for bi in range(nbat):
        rows = slice(bi * Ts, (bi + 1) * Ts)
        k2, v2 = [], []
        for t in range(KV_W // LANES):
            sl = slice(t * LANES, (t + 1) * LANES)
            kcat = jnp.concatenate([ck_ref[bi, :, sl], _rope(kn_ref[rows, sl], c, s1, s2)], axis=0)
            vcat = jnp.concatenate([cv_ref[bi, :, sl], vn_ref[rows, sl]], axis=0)
            ko_ref[bi, :, sl] = kcat[Ts:, :]
            vo_ref[bi, :, sl] = vcat[Ts:, :]
            k2.extend(x.astype(BF16) for x in _dup_heads(kcat))
            v2.extend(x.astype(BF16) for x in _dup_heads(vcat))
        for p in range(N_Q_HEADS // 2):
            sl = slice(p * LANES, (p + 1) * LANES)
            q = _rope(q_ref[rows, sl], c, s1, s2)
            g = p // pairs_per_kv
            o = _pair_attention(q, k2[g], v2[g], mask, sink_ref[2 * p], sink_ref[2 * p + 1])
            o_ref[rows, sl] = (o * g_ref[rows, sl].astype(F32)).astype(o_ref.dtype)


def _attn_sample(qkv, g_a, sinks, cache_k, cache_v, B, Ts):
    W = WINDOW
    nbat = _tile(B, (4, 2, 1))
    tabs = _rope_tables(PAST_LEN + jnp.arange(Ts))
    kcol = ATT_W // KV_W
    rows = nbat * Ts
    cspec = pl.BlockSpec((nbat, W, KV_W), lambda i: (i, 0, 0))
    tspec = pl.BlockSpec((Ts, LANES), lambda i: (0, 0))
    cache = jax.ShapeDtypeStruct((B, W, KV_W), F32)
    return pl.pallas_call(
        _attn_sample_kernel,
        out_shape=(jax.ShapeDtypeStruct((B * Ts, ATT_W), BF16), cache, cache),
        grid=(B // nbat,),
        in_specs=[pl.BlockSpec(memory_space=pltpu.SMEM),
                  pl.BlockSpec((rows, ATT_W), lambda i: (i, 0)),
                  pl.BlockSpec((rows, KV_W), lambda i: (i, kcol)),
                  pl.BlockSpec((rows, KV_W), lambda i: (i, kcol + 1)),
                  cspec, cspec,
                  pl.BlockSpec((rows, ATT_W), lambda i: (i, 0)),
                  tspec, tspec, tspec],
        out_specs=(pl.BlockSpec((rows, ATT_W), lambda i: (i, 0)), cspec, cspec),
        compiler_params=_params("parallel"),
        name="attn_sample",
    )(sinks, qkv, qkv, qkv, cache_k, cache_v, g_a, *tabs)


def _mix_kernel(yr_ref, ya_ref, wr_ref, wa_ref, gr_ref, ga_ref, o_ref):
    br = jnp.dot(yr_ref[...], wr_ref[...], preferred_element_type=F32)
    ba = jnp.dot(ya_ref[...], wa_ref[...], preferred_element_type=F32)
    o_ref[...] = (gr_ref[...].astype(F32) * br + ga_ref[...].astype(F32) * ba).astype(o_ref.dtype)


def _mix(y_r, y_a, w_r, w_a, gates):
    M = y_r.shape[0]
    D = w_r.shape[1]
    tm = _tile(M, (1024, 512, 256, 128))
    tn = _tile(D, (512, 256, 128))
    nj = D // tn
    return pl.pallas_call(
        _mix_kernel,
        out_shape=jax.ShapeDtypeStruct((M, D), BF16),
        grid=(M // tm, nj),
        in_specs=[pl.BlockSpec((tm, RWKV_W), lambda i, j: (i, 0)),
                  pl.BlockSpec((tm, ATT_W), lambda i, j: (i, 0)),
                  pl.BlockSpec((RWKV_W, tn), lambda i, j: (0, j)),
                  pl.BlockSpec((ATT_W, tn), lambda i, j: (0, j)),
                  pl.BlockSpec((tm, tn), lambda i, j: (i, j)),
                  pl.BlockSpec((tm, tn), lambda i, j: (i, j + nj))],
        out_specs=pl.BlockSpec((tm, tn), lambda i, j: (i, j)),
        compiler_params=_params("parallel", "arbitrary"),
        name="branch_mix",
    )(y_r, y_a, w_r, w_a, gates, gates)


def _out_kernel(mix_ref, w_ref, x_ref, g_ref, b_ref, o_ref, *, tn):
    j = pl.program_id(1)
    col = pl.multiple_of(j * tn, tn)
    out = jnp.dot(mix_ref[...], w_ref[...], preferred_element_type=F32)
    o_ref[:, pl.ds(col, tn)] = ALPHA * x_ref[...] + out

    @pl.when(j == pl.num_programs(1) - 1)
    def _():
        h = o_ref[...]
        m = jnp.mean(h, axis=1, keepdims=True)
        d = h - m
        var = jnp.mean(d * d, axis=1, keepdims=True)
        o_ref[...] = d * lax.rsqrt(var + LN_EPS) * g_ref[...] + b_ref[...]


def _out_ln(mix, w_out, x, ln_g, ln_b):
    M, D = x.shape
    tm = _tile(M, (512, 256, 128))
    tn = _tile(D, (512, 256, 128))
    row = pl.BlockSpec((1, D), lambda i, j: (0, 0))
    return pl.pallas_call(
        functools.partial(_out_kernel, tn=tn),
        out_shape=jax.ShapeDtypeStruct((M, D), F32),
        grid=(M // tm, D // tn),
        in_specs=[pl.BlockSpec((tm, D), lambda i, j: (i, 0)),
                  pl.BlockSpec((D, tn), lambda i, j: (0, j)),
                  pl.BlockSpec((tm, tn), lambda i, j: (i, j)),
                  row, row],
        out_specs=pl.BlockSpec((tm, D), lambda i, j: (i, 0)),
        compiler_params=_params("parallel", "arbitrary"),
        name="out_ln",
    )(mix, w_out, x, ln_g, ln_b)


def _layer(x, S0, shift0, caches, w, p):
    B, T, D = x.shape
    M = B * T
    x2 = x.reshape(M, D)
    xb = x2.astype(BF16)
    zs = _matmul(xb, w["zs"], F32, name="proj_shift")
    g_r = _matmul(xb, w["g_r"], BF16, act="silu", name="proj_gate_r")
    qkv = _matmul(xb, w["qkv"], F32, name="proj_qkv")
    g_a = _matmul(xb, w["g_a"], BF16, act="silu", name="proj_gate_a")
    gates = _matmul(xb, w["gates"], BF16, act="sigmoid", name="proj_merge_gates")

    zs3 = zs.reshape(B, T, SHIFT_W)
    prep = _rwkv_prep(zs3, shift0.reshape(B, 1, SHIFT_W), p)
    ops = [_to_head_lanes(a) for a in prep[:6]]
    s0 = S0.transpose(3, 2, 0, 1).reshape(HEAD, HEAD, B * RWKV_HEADS)
    yn, s_fin = _wkv_scan(ops, s0)
    y_r = _rwkv_gate(_from_head_lanes(yn, B).reshape(M, RWKV_W), prep[6].reshape(M, RWKV_W), g_r,
                     p["gn_w"], p["gn_b"])
    S_new = s_fin.reshape(HEAD, HEAD, B, RWKV_HEADS).transpose(2, 3, 1, 0)
    shift_new = zs3[:, T - 1, :]

    if caches is None:
        y_a, k_rot = _attn_prompt(qkv, g_a, p["sinks"], B, T)
        k_buf = k_rot.reshape(B, T, N_KV_HEADS, HEAD)[:, T - WINDOW:]
        v_buf = qkv.reshape(B, T, ATT_W + 2 * KV_W)[:, T - WINDOW:, ATT_W + KV_W:].reshape(B, WINDOW, N_KV_HEADS, HEAD)
    else:
        y_a, k_buf, v_buf = _attn_sample(qkv, g_a, p["sinks"], caches[0], caches[1], B, T)
        k_buf = k_buf.reshape(B, WINDOW, N_KV_HEADS, HEAD)
        v_buf = v_buf.reshape(B, WINDOW, N_KV_HEADS, HEAD)

    mix = _mix(y_r, y_a, w["branch_r"], w["branch_a"], gates)
    y = _out_ln(mix, w["out"], x2, p["ln_g"], p["ln_b"]).reshape(B, T, D)
    return y, S_new, shift_new, k_buf, v_buf


def kernel(x_prompt, x_sample, state_wkv, state_shift, cache_win_k, cache_win_v, w_in, mu_shift, w0, w_decay_up, a0, w_aaa_up, k_k, k_a, r_k, gn_w, gn_b, sinks, w_branch_rwkv, w_branch_attn, w_out, ln_g, ln_b):
    assert w_in.shape[0] == DEPTH
    Bp = x_prompt.shape[0]
    Bs = x_sample.shape[0]
    D = x_prompt.shape[2]
    o_gr = SHIFT_W
    o_q = o_gr + RWKV_W
    o_ga = o_q + ATT_W + 2 * KV_W
    o_m = o_ga + ATT_W
    wb = w_in[0].astype(BF16)
    w = {"zs": wb[:, :o_gr], "g_r": wb[:, o_gr:o_q], "qkv": wb[:, o_q:o_ga], "g_a": wb[:, o_ga:o_m],
         "gates": wb[:, o_m:], "branch_r": w_branch_rwkv[0].astype(BF16),
         "branch_a": w_branch_attn[0].astype(BF16), "out": w_out[0].astype(BF16)}
    lane = jnp.arange(LANES)
    p = {"mu": mu_shift, "w0": w0, "wup": w_decay_up[0].astype(BF16), "a0": a0,
         "aup": w_aaa_up[0].astype(BF16), "k_k": k_k, "k_a": k_a, "r_k": r_k.reshape(1, RWKV_W),
         "gn_w": gn_w, "gn_b": gn_b, "sinks": sinks[0], "ln_g": ln_g, "ln_b": ln_b,
         "jmat": (lane[:, None] // HEAD == lane[None, :] // HEAD).astype(BF16)}

    S0p = jnp.zeros((Bp, RWKV_HEADS, HEAD, HEAD), F32)
    sh0p = jnp.zeros((Bp, SHIFT_W), F32)
    yp, a1, a2, a3, a4 = _layer(x_prompt, S0p, sh0p, None, w, p)
    caches = (cache_win_k[0].reshape(Bs, WINDOW, KV_W), cache_win_v[0].reshape(Bs, WINDOW, KV_W))
    ys, b1, b2, b3, b4 = _layer(x_sample, state_wkv[0], state_shift[0], caches, w, p)
    return (yp, ys, a1[None], a2[None], a3[None], a4[None], b1[None], b2[None], b3[None], b4[None])
```

```python
import functools
import math

import jax
import jax.numpy as jnp
from jax import lax
from jax.experimental import pallas as pl
from jax.experimental.pallas import tpu as pltpu

RWKV_HEADS = 32
HEAD = 64
RWKV_W = RWKV_HEADS * HEAD
LORA = 128
SHIFT_W = 3 * RWKV_W + 2 * LORA
GN_EPS = 64e-5
N_Q_HEADS = 32
N_KV_HEADS = 4
ATT_W = N_Q_HEADS * HEAD
KV_W = N_KV_HEADS * HEAD
WINDOW = 128
ROT_DIM = HEAD // 4
ROPE_THETA = 500000.0
PAST_LEN = 8192
DEPTH = 1
ALPHA = (2.0 * DEPTH) ** 0.25
LN_EPS = 1e-5
LANES = 128
QUAD = LANES // RWKV_HEADS
DECAY_SCALE = math.exp(-0.5)
VMEM_LIMIT = 56 * 1024 * 1024

BF16 = jnp.bfloat16
F32 = jnp.float32


def _params(*sem):
    return pltpu.CompilerParams(dimension_semantics=sem, vmem_limit_bytes=VMEM_LIMIT)


def _tile(n, prefs):
    for p in prefs:
        if n % p == 0:
            return p
    return n


def _sigmoid(x):
    return 1.0 / (1.0 + jnp.exp(-x))


def _mm_kernel(x_ref, w_ref, o_ref, *, act):
    acc = jnp.dot(x_ref[...], w_ref[...], preferred_element_type=F32)
    if act == "silu":
        acc = acc * _sigmoid(acc)
    elif act == "sigmoid":
        acc = _sigmoid(acc)
    o_ref[...] = acc.astype(o_ref.dtype)


def _matmul(x, w, out_dtype, act=None, name="proj"):
    M, K = x.shape
    N = w.shape[1]
    tm = _tile(M, (1024, 512, 256, 128))
    tn = _tile(N, (512, 640, 256, 128))
    return pl.pallas_call(
        functools.partial(_mm_kernel, act=act),
        out_shape=jax.ShapeDtypeStruct((M, N), out_dtype),
        grid=(M // tm, N // tn),
        in_specs=[pl.BlockSpec((tm, K), lambda i, j: (i, 0)),
                  pl.BlockSpec((K, tn), lambda i, j: (0, j))],
        out_specs=pl.BlockSpec((tm, tn), lambda i, j: (i, j)),
        compiler_params=_params("parallel", "arbitrary"),
        name=name,
    )(x, w)


def _head_sum(x, j_ref):
    nt = x.shape[1] // LANES
    t = x[:, 0:LANES]
    for m in range(1, nt):
        t = t + x[:, m * LANES:(m + 1) * LANES]
    hi = t.astype(BF16)
    lo = (t - hi.astype(F32)).astype(BF16)
    s = (jnp.dot(hi, j_ref[...], preferred_element_type=F32)
         + jnp.dot(lo, j_ref[...], preferred_element_type=F32))
    return jnp.concatenate([s] * nt, axis=1)


def _lane_quarter_select(parts):
    lane = lax.broadcasted_iota(jnp.int32, parts[0].shape, 1)
    out = parts[QUAD - 1]
    for q in range(QUAD - 2, -1, -1):
        out = jnp.where(lane < (q + 1) * RWKV_HEADS, parts[q], out)
    return out


def _roll_lanes(x, shift):
    shift = shift % LANES
    return x if shift == 0 else pltpu.roll(x, shift, axis=1)


def _interleave_store(xs, o_ref):
    for m in range(RWKV_W // LANES):
        tiles = [x[:, m * LANES:(m + 1) * LANES] for x in xs]
        for j in range(QUAD):
            parts = [_roll_lanes(t, (b - j) * RWKV_HEADS) for b, t in enumerate(tiles)]
            k = m * QUAD + j
            o_ref[0, :, k * LANES:(k + 1) * LANES] = _lane_quarter_select(parts)


def _prep_kernel(zs_ref, sh_ref, mu_ref, w0_ref, wup_ref, a0_ref, aup_ref, kk_ref, ka_ref, rk_ref, j_ref,
                 r_o, w_o, kap_o, b_o, kt_o, v_o, bon_o, carry):
    @pl.when(pl.program_id(1) == 0)
    def _():
        carry[...] = sh_ref[...]

    tb = zs_ref.shape[1]
    row = lax.broadcasted_iota(jnp.int32, (tb, 1), 0)
    outs = [[] for _ in range(6)]
    for bq in range(QUAD):
        zs = zs_ref[bq]
        rolled = pltpu.roll(zs, 1, axis=0)
        prev = jnp.where(row == 0, carry[bq], rolled)
        carry[bq] = zs[tb - 1:tb, :]
        xs = zs + mu_ref[...] * (prev - zs)
        r = xs[:, 0:RWKV_W]
        k = xs[:, RWKV_W:2 * RWKV_W]
        v = xs[:, 2 * RWKV_W:3 * RWKV_W]
        wd = xs[:, 3 * RWKV_W:3 * RWKV_W + LORA]
        ad = xs[:, 3 * RWKV_W + LORA:SHIFT_W]
        wl = w0_ref[...] + jnp.dot(jnp.tanh(wd).astype(BF16), wup_ref[...], preferred_element_type=F32)
        dec = jnp.exp(-DECAY_SCALE * _sigmoid(wl))
        a = _sigmoid(a0_ref[...] + jnp.dot(ad.astype(BF16), aup_ref[...], preferred_element_type=F32))
        kkv = k * kk_ref[...]
        norm = jnp.maximum(jnp.sqrt(_head_sum(kkv * kkv, j_ref)), 1e-12)
        kap = kkv / norm
        kt = k * (1.0 + (a - 1.0) * ka_ref[...])
        bon_o[bq] = _head_sum(r * kt * rk_ref[...], j_ref) * v
        for lst, val in zip(outs, (r, dec, kap, kap * a, kt, v)):
            lst.append(val)
    for lst, o_ref in zip(outs, (r_o, w_o, kap_o, b_o, kt_o, v_o)):
        _interleave_store(lst, o_ref)


def _rwkv_prep(zs, shift0, p):
    B, T, _ = zs.shape
    G = B // QUAD
    tb = _tile(T, (32, 16, 8))
    row = lambda n: pl.BlockSpec((1, n), lambda g, i: (0, 0))
    full = lambda a: pl.BlockSpec(a.shape, lambda g, i: (0, 0))
    op = jax.ShapeDtypeStruct((G, T, HEAD * LANES), F32)
    opspec = pl.BlockSpec((1, tb, HEAD * LANES), lambda g, i: (g, i, 0))
    return pl.pallas_call(
        _prep_kernel,
        out_shape=(op,) * 6 + (jax.ShapeDtypeStruct((B, T, RWKV_W), F32),),
        grid=(G, T // tb),
        in_specs=[pl.BlockSpec((QUAD, tb, SHIFT_W), lambda g, i: (g, i, 0)),
                  pl.BlockSpec((QUAD, 1, SHIFT_W), lambda g, i: (g, 0, 0)),
                  row(SHIFT_W), row(RWKV_W), full(p["wup"]), row(RWKV_W), full(p["aup"]),
                  row(RWKV_W), row(RWKV_W), row(RWKV_W), full(p["jmat"])],
        out_specs=(opspec,) * 6 + (pl.BlockSpec((QUAD, tb, RWKV_W), lambda g, i: (g, i, 0)),),
        scratch_shapes=[pltpu.VMEM((QUAD, 1, SHIFT_W), F32)],
        compiler_params=_params("parallel", "arbitrary"),
        name="rwkv_prep",
    )(zs, shift0, p["mu"], p["w0"], p["wup"], p["a0"], p["aup"], p["k_k"], p["k_a"], p["r_k"], p["jmat"])


def _scan_kernel(r_ref, w_ref, kap_ref, b_ref, kt_ref, v_ref, s0_ref, y_ref, sout_ref, S):
    ti = pl.program_id(1)

    @pl.when(ti == 0)
    def _():
        S[...] = s0_ref[...]

    tb = r_ref.shape[1]

    u0 = S[0] * kap_ref[0, 0, 0:1, :]
    for k in range(1, HEAD):
        u0 = u0 + S[k] * kap_ref[0, 0, k:k + 1, :]

    def step(t, u):
        tn = jnp.minimum(t + 1, tb - 1)
        vt = v_ref[0, t]
        y = None
        un = None
        for k in range(HEAD):
            s = S[k] * w_ref[0, t, k:k + 1, :] - u * b_ref[0, t, k:k + 1, :] + vt * kt_ref[0, t, k:k + 1, :]
            S[k] = s
            ty = s * r_ref[0, t, k:k + 1, :]
            tu = s * kap_ref[0, tn, k:k + 1, :]
            y = ty if y is None else y + ty
            un = tu if un is None else un + tu
        m = jnp.mean(y, axis=0, keepdims=True)
        d = y - m
        var = jnp.mean(d * d, axis=0, keepdims=True)
        y_ref[0, t] = d * lax.rsqrt(var + GN_EPS)
        return un

    lax.fori_loop(0, tb, step, u0)

    @pl.when(ti == pl.num_programs(1) - 1)
    def _():
        sout_ref[...] = S[...]


def _wkv_scan(ops, s0):
    G, T = ops[0].shape[:2]
    tb = _tile(T, (32, 16, 8))
    ospec = pl.BlockSpec((1, tb, HEAD, LANES), lambda g, i: (g, i, 0, 0))
    sspec = pl.BlockSpec((HEAD, HEAD, LANES), lambda g, i: (0, 0, g))
    return pl.pallas_call(
        _scan_kernel,
        out_shape=(jax.ShapeDtypeStruct((G, T, HEAD, LANES), F32),
                   jax.ShapeDtypeStruct((HEAD, HEAD, G * LANES), F32)),
        grid=(G, T // tb),
        in_specs=[ospec] * 6 + [sspec],
        out_specs=(ospec, sspec),
        scratch_shapes=[pltpu.VMEM((HEAD, HEAD, LANES), F32)],
        compiler_params=_params("parallel", "arbitrary"),
        name="wkv_scan",
    )(*ops, s0)


def _gate_kernel(yn_ref, bon_ref, g_ref, gw_ref, gb_ref, o_ref):
    for m in range(RWKV_W // LANES):
        tiles = [yn_ref[0, :, (m * QUAD + j) * LANES:(m * QUAD + j + 1) * LANES] for j in range(QUAD)]
        sl = slice(m * LANES, (m + 1) * LANES)
        for bq in range(QUAD):
            yn = _lane_quarter_select([_roll_lanes(t, (j - bq) * RWKV_HEADS) for j, t in enumerate(tiles)])
            y = yn * gw_ref[:, sl] + gb_ref[:, sl] + bon_ref[bq, :, sl]
            o_ref[bq, :, sl] = (y * g_ref[bq, :, sl].astype(F32)).astype(o_ref.dtype)


def _rwkv_gate(yn, bonus, g, gn_w, gn_b):
    B, T, _ = bonus.shape
    G = B // QUAD
    tb = _tile(T, (128, 64, 32, 16, 8))
    spec = pl.BlockSpec((QUAD, tb, RWKV_W), lambda g_, i: (g_, i, 0))
    row = pl.BlockSpec((1, RWKV_W), lambda g_, i: (0, 0))
    return pl.pallas_call(
        _gate_kernel,
        out_shape=jax.ShapeDtypeStruct((B, T, RWKV_W), BF16),
        grid=(G, T // tb),
        in_specs=[pl.BlockSpec((1, tb, HEAD * LANES), lambda g_, i: (g_, i, 0)), spec, spec, row, row],
        out_specs=spec,
        compiler_params=_params("parallel", "parallel"),
        name="rwkv_gate",
    )(yn, bonus, g, gn_w, gn_b)


def _rope_tables(pos):
    half = ROT_DIM // 2
    inv = ROPE_THETA ** (-jnp.arange(half, dtype=F32) * 2.0 / ROT_DIM)
    ang = pos.astype(F32)[:, None] * inv[None, :]
    cos, sin = jnp.cos(ang), jnp.sin(ang)
    n = pos.shape[0]
    one = jnp.ones((n, HEAD - ROT_DIM), F32)
    zero = jnp.zeros((n, HEAD - ROT_DIM), F32)
    zh = jnp.zeros((n, half), F32)
    c = jnp.concatenate([cos, cos, one], axis=1)
    s1 = jnp.concatenate([zh, sin, zero], axis=1)
    s2 = jnp.concatenate([-sin, zh, zero], axis=1)
    reps = LANES // HEAD
    return jnp.tile(c, (1, reps)), jnp.tile(s1, (1, reps)), jnp.tile(s2, (1, reps))


def _rope(x, c, s1, s2):
    half = ROT_DIM // 2
    return x * c + pltpu.roll(x, half, axis=1) * s1 + pltpu.roll(x, LANES - half, axis=1) * s2


def _dup_heads(x):
    sw = pltpu.roll(x, HEAD, axis=1)
    low = lax.broadcasted_iota(jnp.int32, x.shape, 1) < HEAD
    return jnp.where(low, x, sw), jnp.where(low, sw, x)


def _pair_attention(q, k2, v2, mask, sink0, sink1):
    n = q.shape[0]
    low = lax.broadcasted_iota(jnp.int32, q.shape, 1) < HEAD
    qs = jnp.concatenate([jnp.where(low, q, 0.0), jnp.where(low, 0.0, q)], axis=0).astype(BF16)
    s = lax.dot_general(qs, k2, (((1,), (1,)), ((), ())), preferred_element_type=F32)
    s = s * (1.0 / math.sqrt(HEAD))
    s = jnp.where(mask, s, -1e30)
    top = lax.broadcasted_iota(jnp.int32, (2 * n, 1), 0) < n
    sink = jnp.where(top, sink0, sink1)
    m = jnp.maximum(jnp.max(s, axis=1, keepdims=True), sink)
    e = jnp.exp(s - m)
    den = jnp.sum(e, axis=1, keepdims=True) + jnp.exp(sink - m)
    p = (e / den).astype(BF16)
    o = jnp.dot(p, v2, preferred_element_type=F32)
    return jnp.where(low, o[:n], o[n:])


def _attn_prompt_kernel(sink_ref, q_ref, kc_ref, kp_ref, vc_ref, vp_ref, g_ref,
                        cc_ref, s1c_ref, s2c_ref, cp_ref, s1p_ref, s2p_ref, o_ref, krot_ref):
    i = pl.program_id(1)
    W = WINDOW
    qi = lax.broadcasted_iota(jnp.int32, (2 * W, 2 * W), 0) % W
    kj = lax.broadcasted_iota(jnp.int32, (2 * W, 2 * W), 1)
    rel = W + qi - kj
    mask = (rel >= 0) & (rel < W) & ((i > 0) | (kj >= W))
    cc, s1c, s2c = cc_ref[...], s1c_ref[...], s2c_ref[...]
    cp, s1p, s2p = cp_ref[...], s1p_ref[...], s2p_ref[...]
    k2, v2 = [], []
    for t in range(KV_W // LANES):
        sl = slice(t * LANES, (t + 1) * LANES)
        kc = _rope(kc_ref[:, sl], cc, s1c, s2c)
        krot_ref[:, sl] = kc
        kcat = jnp.concatenate([_rope(kp_ref[:, sl], cp, s1p, s2p), kc], axis=0)
        vcat = jnp.concatenate([vp_ref[:, sl], vc_ref[:, sl]], axis=0)
        k2.extend(x.astype(BF16) for x in _dup_heads(kcat))
        v2.extend(x.astype(BF16) for x in _dup_heads(vcat))
    pairs_per_kv = N_Q_HEADS // N_KV_HEADS // 2
    for p in range(N_Q_HEADS // 2):
        sl = slice(p * LANES, (p + 1) * LANES)
        q = _rope(q_ref[:, sl], cc, s1c, s2c)
        g = p // pairs_per_kv
        o = _pair_attention(q, k2[g], v2[g], mask, sink_ref[2 * p], sink_ref[2 * p + 1])
        o_ref[:, sl] = (o * g_ref[:, sl].astype(F32)).astype(o_ref.dtype)


def _attn_prompt(qkv, g_a, sinks, B, T):
    W = WINDOW
    nb = T // W
    tabs = _rope_tables(jnp.arange(T))
    kcol = ATT_W // KV_W
    cur = lambda b, i: (b * nb + i, 0)
    tcur = pl.BlockSpec((W, LANES), lambda b, i: (i, 0))
    tprv = pl.BlockSpec((W, LANES), lambda b, i: (jnp.maximum(i - 1, 0), 0))
    return pl.pallas_call(
        _attn_prompt_kernel,
        out_shape=(jax.ShapeDtypeStruct((B * T, ATT_W), BF16), jax.ShapeDtypeStruct((B * T, KV_W), F32)),
        grid=(B, nb),
        in_specs=[pl.BlockSpec(memory_space=pltpu.SMEM),
                  pl.BlockSpec((W, ATT_W), cur),
                  pl.BlockSpec((W, KV_W), lambda b, i: (b * nb + i, kcol)),
                  pl.BlockSpec((W, KV_W), lambda b, i: (b * nb + jnp.maximum(i - 1, 0), kcol)),
                  pl.BlockSpec((W, KV_W), lambda b, i: (b * nb + i, kcol + 1)),
                  pl.BlockSpec((W, KV_W), lambda b, i: (b * nb + jnp.maximum(i - 1, 0), kcol + 1)),
                  pl.BlockSpec((W, ATT_W), cur),
                  tcur, tcur, tcur, tprv, tprv, tprv],
        out_specs=(pl.BlockSpec((W, ATT_W), cur), pl.BlockSpec((W, KV_W), cur)),
        compiler_params=_params("parallel", "parallel"),
        name="attn_prompt",
    )(sinks, qkv, qkv, qkv, qkv, qkv, g_a, *tabs, *tabs)


def _attn_sample_kernel(q_ref, kn_ref, vn_ref, ck_ref, cv_ref, g_ref, sink_ref, c_ref, s1_ref, s2_ref,
                        o_ref, ko_ref, vo_ref):
    W = WINDOW
    nbat = ck_ref.shape[0]
    Ts = q_ref.shape[0] // nbat
    GQ = N_Q_HEADS // N_KV_HEADS
    GW = GQ * HEAD
    nrow = N_Q_HEADS * Ts
    tq = lax.broadcasted_iota(jnp.int32, (nrow, W + Ts), 0) % Ts
    kj = lax.broadcasted_iota(jnp.int32, (nrow, W + Ts), 1)
    rel = W + tq - kj
    mask = (rel >= 0) & (rel < W)
    sink = sink_ref[...]
    lane_head = lax.broadcasted_iota(jnp.int32, (Ts, GW), 1) // HEAD
    c, s1, s2 = c_ref[...], s1_ref[...], s2_ref[...]
    tiles_per_group = GW // LANES
    for bi in range(nbat):
        rows = slice(bi * Ts, (bi + 1) * Ts)
        k2, v2 = [], []
        for t in range(KV_W // LANES):
            sl = slice(t * LANES, (t + 1) * LANES)
            kcat = jnp.concatenate([ck_ref[bi, :, sl], _rope(kn_ref[rows, sl], c, s1, s2)], axis=0)
            vcat = jnp.concatenate([cv_ref[bi, :, sl], vn_ref[rows, sl]], axis=0)
            ko_ref[bi, :, sl] = kcat[Ts:, :]
            vo_ref[bi, :, sl] = vcat[Ts:, :]
            k2.extend(x.astype(BF16) for x in _dup_heads(kcat))
            v2.extend(x.astype(BF16) for x in _dup_heads(vcat))
        s_parts = []
        for g in range(N_KV_HEADS):
            qg = jnp.concatenate(
                [_rope(q_ref[rows, (g * tiles_per_group + i) * LANES:(g * tiles_per_group + i + 1) * LANES], c, s1, s2)
                 for i in range(tiles_per_group)], axis=1)
            qs = jnp.concatenate([jnp.where(lane_head == h, qg, 0.0) for h in range(GQ)], axis=0).astype(BF16)
            k8 = jnp.concatenate([k2[g]] * tiles_per_group, axis=1)
            s_parts.append(lax.dot_general(qs, k8, (((1,), (1,)), ((), ())), preferred_element_type=F32))
        s = jnp.concatenate(s_parts, axis=0) * (1.0 / math.sqrt(HEAD))
        s = jnp.where(mask, s, -1e30)
        m = jnp.maximum(jnp.max(s, axis=1, keepdims=True), sink)
        e = jnp.exp(s - m)
        den = jnp.sum(e, axis=1, keepdims=True) + jnp.exp(sink - m)
        p = (e / den).astype(BF16)
        for g in range(N_KV_HEADS):
            v8 = jnp.concatenate([v2[g]] * tiles_per_group, axis=1)
            o = jnp.dot(p[g * GQ * Ts:(g + 1) * GQ * Ts], v8, preferred_element_type=F32)
            og = o[0:Ts]
            for h in range(1, GQ):
                og = jnp.where(lane_head == h, o[h * Ts:(h + 1) * Ts], og)
            gl = slice(g * GW, (g + 1) * GW)
            o_ref[rows, gl] = (og * g_ref[rows, gl].astype(F32)).astype(o_ref.dtype)


def _attn_sample(qkv, g_a, sinks, cache_k, cache_v, B, Ts):
    W = WINDOW
    nbat = _tile(B, (4, 2, 1))
    tabs = _rope_tables(PAST_LEN + jnp.arange(Ts))
    kcol = ATT_W // KV_W
    rows = nbat * Ts
    sink_col = jnp.repeat(sinks, Ts).reshape(N_Q_HEADS * Ts, 1)
    cspec = pl.BlockSpec((nbat, W, KV_W), lambda i: (i, 0, 0))
    tspec = pl.BlockSpec((Ts, LANES), lambda i: (0, 0))
    cache = jax.ShapeDtypeStruct((B, W, KV_W), F32)
    return pl.pallas_call(
        _attn_sample_kernel,
        out_shape=(jax.ShapeDtypeStruct((B * Ts, ATT_W), BF16), cache, cache),
        grid=(B // nbat,),
        in_specs=[pl.BlockSpec((rows, ATT_W), lambda i: (i, 0)),
                  pl.BlockSpec((rows, KV_W), lambda i: (i, kcol)),
                  pl.BlockSpec((rows, KV_W), lambda i: (i, kcol + 1)),
                  cspec, cspec,
                  pl.BlockSpec((rows, ATT_W), lambda i: (i, 0)),
                  pl.BlockSpec((N_Q_HEADS * Ts, 1), lambda i: (0, 0)),
                  tspec, tspec, tspec],
        out_specs=(pl.BlockSpec((rows, ATT_W), lambda i: (i, 0)), cspec, cspec),
        compiler_params=_params("parallel"),
        name="attn_sample",
    )(qkv, qkv, qkv, cache_k, cache_v, g_a, sink_col, *tabs)


def _mix_kernel(yr_ref, ya_ref, wr_ref, wa_ref, gr_ref, ga_ref, o_ref):
    br = jnp.dot(yr_ref[...], wr_ref[...], preferred_element_type=F32)
    ba = jnp.dot(ya_ref[...], wa_ref[...], preferred_element_type=F32)
    o_ref[...] = (gr_ref[...].astype(F32) * br + ga_ref[...].astype(F32) * ba).astype(o_ref.dtype)


def _mix(y_r, y_a, w_r, w_a, gates):
    M = y_r.shape[0]
    D = w_r.shape[1]
    tm = _tile(M, (1024, 512, 256, 128))
    tn = _tile(D, (512, 256, 128))
    nj = D // tn
    return pl.pallas_call(
        _mix_kernel,
        out_shape=jax.ShapeDtypeStruct((M, D), BF16),
        grid=(M // tm, nj),
        in_specs=[pl.BlockSpec((tm, RWKV_W), lambda i, j: (i, 0)),
                  pl.BlockSpec((tm, ATT_W), lambda i, j: (i, 0)),
                  pl.BlockSpec((RWKV_W, tn), lambda i, j: (0, j)),
                  pl.BlockSpec((ATT_W, tn), lambda i, j: (0, j)),
                  pl.BlockSpec((tm, tn), lambda i, j: (i, j)),
                  pl.BlockSpec((tm, tn), lambda i, j: (i, j + nj))],
        out_specs=pl.BlockSpec((tm, tn), lambda i, j: (i, j)),
        compiler_params=_params("parallel", "arbitrary"),
        name="branch_mix",
    )(y_r, y_a, w_r, w_a, gates, gates)


def _out_kernel(mix_ref, w_ref, x_ref, g_ref, b_ref, o_ref, *, tn):
    j = pl.program_id(1)
    col = pl.multiple_of(j * tn, tn)
    out = jnp.dot(mix_ref[...], w_ref[...], preferred_element_type=F32)
    o_ref[:, pl.ds(col, tn)] = ALPHA * x_ref[...] + out

    @pl.when(j == pl.num_programs(1) - 1)
    def _():
        h = o_ref[...]
        m = jnp.mean(h, axis=1, keepdims=True)
        d = h - m
        var = jnp.mean(d * d, axis=1, keepdims=True)
        o_ref[...] = d * lax.rsqrt(var + LN_EPS) * g_ref[...] + b_ref[...]


def _out_ln(mix, w_out, x, ln_g, ln_b):
    M, D = x.shape
    tm = _tile(M, (512, 256, 128))
    tn = _tile(D, (512, 256, 128))
    row = pl.BlockSpec((1, D), lambda i, j: (0, 0))
    return pl.pallas_call(
        functools.partial(_out_kernel, tn=tn),
        out_shape=jax.ShapeDtypeStruct((M, D), F32),
        grid=(M // tm, D // tn),
        in_specs=[pl.BlockSpec((tm, D), lambda i, j: (i, 0)),
                  pl.BlockSpec((D, tn), lambda i, j: (0, j)),
                  pl.BlockSpec((tm, tn), lambda i, j: (i, j)),
                  row, row],
        out_specs=pl.BlockSpec((tm, D), lambda i, j: (i, 0)),
        compiler_params=_params("parallel", "arbitrary"),
        name="out_ln",
    )(mix, w_out, x, ln_g, ln_b)


def _k_major(x, axis):
    shp = x.shape
    n = shp[axis] // RWKV_W
    y = x.reshape(shp[:axis] + (n, RWKV_HEADS, HEAD) + shp[axis + 1:])
    y = jnp.swapaxes(y, axis + 1, axis + 2)
    return y.reshape(shp)


def _head_major(x, axis):
    shp = x.shape
    n = shp[axis] // RWKV_W
    y = x.reshape(shp[:axis] + (n, HEAD, RWKV_HEADS) + shp[axis + 1:])
    y = jnp.swapaxes(y, axis + 1, axis + 2)
    return y.reshape(shp)


def _shift_k_major(x):
    return jnp.concatenate([_k_major(x[..., :3 * RWKV_W], x.ndim - 1), x[..., 3 * RWKV_W:]], axis=-1)


def _shift_head_major(x):
    return jnp.concatenate([_head_major(x[..., :3 * RWKV_W], x.ndim - 1), x[..., 3 * RWKV_W:]], axis=-1)


def _layer(x, S0, shift0, caches, w, p):
    B, T, D = x.shape
    M = B * T
    G = B // QUAD
    x2 = x.reshape(M, D)
    xb = x2.astype(BF16)
    zs = _matmul(xb, w["zs"], F32, name="proj_shift")
    g_r = _matmul(xb, w["g_r"], BF16, act="silu", name="proj_gate_r")
    qkv = _matmul(xb, w["qkv"], F32, name="proj_qkv")
    g_a = _matmul(xb, w["g_a"], BF16, act="silu", name="proj_gate_a")
    gates = _matmul(xb, w["gates"], BF16, act="sigmoid", name="proj_merge_gates")

    zs3 = zs.reshape(B, T, SHIFT_W)
    prep = _rwkv_prep(zs3, _shift_k_major(shift0).reshape(B, 1, SHIFT_W), p)
    ops = [a.reshape(G, T, HEAD, LANES) for a in prep[:6]]
    s0 = S0.transpose(3, 2, 0, 1).reshape(HEAD, HEAD, B * RWKV_HEADS)
    yn, s_fin = _wkv_scan(ops, s0)
    y_r = _rwkv_gate(yn.reshape(G, T, HEAD * LANES), prep[6], g_r.reshape(B, T, RWKV_W),
                     p["gn_w"], p["gn_b"]).reshape(M, RWKV_W)
    S_new = s_fin.reshape(HEAD, HEAD, B, RWKV_HEADS).transpose(2, 3, 1, 0)
    shift_new = _shift_head_major(zs3[:, T - 1, :])

    if caches is None:
        y_a, k_rot = _attn_prompt(qkv, g_a, p["sinks"], B, T)
        k_buf = k_rot.reshape(B, T, N_KV_HEADS, HEAD)[:, T - WINDOW:]
        v_buf = qkv.reshape(B, T, ATT_W + 2 * KV_W)[:, T - WINDOW:, ATT_W + KV_W:].reshape(B, WINDOW, N_KV_HEADS, HEAD)
    else:
        y_a, k_buf, v_buf = _attn_sample(qkv, g_a, p["sinks"], caches[0], caches[1], B, T)
        k_buf = k_buf.reshape(B, WINDOW, N_KV_HEADS, HEAD)
        v_buf = v_buf.reshape(B, WINDOW, N_KV_HEADS, HEAD)

    mix = _mix(y_r, y_a, w["branch_r"], w["branch_a"], gates)
    y = _out_ln(mix, w["out"], x2, p["ln_g"], p["ln_b"]).reshape(B, T, D)
    return y, S_new, shift_new, k_buf, v_buf


def kernel(x_prompt, x_sample, state_wkv, state_shift, cache_win_k, cache_win_v, w_in, mu_shift, w0, w_decay_up, a0, w_aaa_up, k_k, k_a, r_k, gn_w, gn_b, sinks, w_branch_rwkv, w_branch_attn, w_out, ln_g, ln_b):
    assert w_in.shape[0] == DEPTH
    Bp = x_prompt.shape[0]
    Bs = x_sample.shape[0]
    assert Bp % QUAD == 0 and Bs % QUAD == 0
    o_gr = SHIFT_W
    o_q = o_gr + RWKV_W
    o_ga = o_q + ATT_W + 2 * KV_W
    o_m = o_ga + ATT_W
    wi = w_in[0]
    w = {"zs": _shift_k_major(wi[:, :o_gr]).astype(BF16),
         "g_r": _k_major(wi[:, o_gr:o_q], 1).astype(BF16),
         "qkv": wi[:, o_q:o_ga].astype(BF16), "g_a": wi[:, o_ga:o_m].astype(BF16),
         "gates": wi[:, o_m:].astype(BF16),
         "branch_r": _k_major(w_branch_rwkv[0], 0).astype(BF16),
         "branch_a": w_branch_attn[0].astype(BF16), "out": w_out[0].astype(BF16)}
    lane = jnp.arange(LANES)
    km = lambda a: _k_major(a.reshape(1, RWKV_W), 1)
    p = {"mu": _shift_k_major(mu_shift), "w0": km(w0), "wup": _k_major(w_decay_up[0], 1).astype(BF16),
         "a0": km(a0), "aup": _k_major(w_aaa_up[0], 1).astype(BF16), "k_k": km(k_k), "k_a": km(k_a),
         "r_k": km(r_k), "gn_w": km(gn_w), "gn_b": km(gn_b), "sinks": sinks[0], "ln_g": ln_g, "ln_b": ln_b,
         "jmat": (lane[:, None] % RWKV_HEADS == lane[None, :] % RWKV_HEADS).astype(BF16)}

    S0p = jnp.zeros((Bp, RWKV_HEADS, HEAD, HEAD), F32)
    sh0p = jnp.zeros((Bp, SHIFT_W), F32)
    yp, a1, a2, a3, a4 = _layer(x_prompt, S0p, sh0p, None, w, p)
    caches = (cache_win_k[0].reshape(Bs, WINDOW, KV_W), cache_win_v[0].reshape(Bs, WINDOW, KV_W))
    ys, b1, b2, b3, b4 = _layer(x_sample, state_wkv[0], state_shift[0], caches, w, p)
    return (yp, ys, a1[None], a2[None], a3[None], a4[None], b1[None], b2[None], b3[None], b4[None])
```

```python
import functools
import math

import jax
import jax.numpy as jnp
from jax import lax
from jax.experimental import pallas as pl
from jax.experimental.pallas import tpu as pltpu

RWKV_HEADS = 32
HEAD = 64
RWKV_W = RWKV_HEADS * HEAD
LORA = 128
SHIFT_W = 3 * RWKV_W + 2 * LORA
GN_EPS = 64e-5
N_Q_HEADS = 32
N_KV_HEADS = 4
ATT_W = N_Q_HEADS * HEAD
KV_W = N_KV_HEADS * HEAD
WINDOW = 128
ROT_DIM = HEAD // 4
ROPE_THETA = 500000.0
PAST_LEN = 8192
DEPTH = 1
ALPHA = (2.0 * DEPTH) ** 0.25
LN_EPS = 1e-5
LANES = 128
QUAD = LANES // RWKV_HEADS
DECAY_SCALE = math.exp(-0.5)
LOG2E = math.log2(math.e)
QK_SCALE = LOG2E / math.sqrt(HEAD)
ZSKV_W = SHIFT_W + 2 * KV_W
F_GA = 0
F_GR = F_GA + ATT_W
F_Q = F_GR + RWKV_W
F_M = F_Q + ATT_W
VMEM_LIMIT = 56 * 1024 * 1024

BF16 = jnp.bfloat16
F32 = jnp.float32


def _params(*sem):
    return pltpu.CompilerParams(dimension_semantics=sem, vmem_limit_bytes=VMEM_LIMIT)


def _tile(n, prefs):
    for p in prefs:
        if n % p == 0:
            return p
    return n


def _sigmoid(x):
    return 1.0 / (1.0 + jnp.exp(-x))


def _silu(x):
    return x * _sigmoid(x)


def _mm_kernel(x_ref, w_ref, o_ref):
    o_ref[...] = jnp.dot(x_ref[...], w_ref[...], preferred_element_type=F32).astype(o_ref.dtype)


def _matmul(x, w, out_dtype, name="proj"):
    M, K = x.shape
    N = w.shape[1]
    tm = _tile(M, (1024, 512, 256, 128))
    tn = _tile(N, (512, 768, 640, 256, 128))
    return pl.pallas_call(
        _mm_kernel,
        out_shape=jax.ShapeDtypeStruct((M, N), out_dtype),
        grid=(M // tm, N // tn),
        in_specs=[pl.BlockSpec((tm, K), lambda i, j: (i, 0)),
                  pl.BlockSpec((K, tn), lambda i, j: (0, j))],
        out_specs=pl.BlockSpec((tm, tn), lambda i, j: (i, j)),
        compiler_params=_params("parallel", "arbitrary"),
        name=name,
    )(x, w)


def _head_sum(x, j_ref):
    nt = x.shape[1] // LANES
    t = x[:, 0:LANES]
    for m in range(1, nt):
        t = t + x[:, m * LANES:(m + 1) * LANES]
    hi = t.astype(BF16)
    lo = (t - hi.astype(F32)).astype(BF16)
    s = (jnp.dot(hi, j_ref[...], preferred_element_type=F32)
         + jnp.dot(lo, j_ref[...], preferred_element_type=F32))
    return jnp.concatenate([s] * nt, axis=1)


def _lane_quarter_select(parts):
    lane = lax.broadcasted_iota(jnp.int32, parts[0].shape, 1)
    out = parts[QUAD - 1]
    for q in range(QUAD - 2, -1, -1):
        out = jnp.where(lane < (q + 1) * RWKV_HEADS, parts[q], out)
    return out


def _roll_lanes(x, shift):
    shift = shift % LANES
    return x if shift == 0 else pltpu.roll(x, shift, axis=1)


def _interleave_store(xs, o_ref):
    for m in range(RWKV_W // LANES):
        tiles = [x[:, m * LANES:(m + 1) * LANES] for x in xs]
        for j in range(QUAD):
            parts = [_roll_lanes(t, (b - j) * RWKV_HEADS) for b, t in enumerate(tiles)]
            k = m * QUAD + j
            o_ref[0, :, k * LANES:(k + 1) * LANES] = _lane_quarter_select(parts)


def _prep_kernel(zs_ref, sh_ref, mu_ref, w0_ref, wup_ref, a0_ref, aup_ref, kk_ref, ka_ref, rk_ref, j_ref,
                 r_o, w_o, kap_o, b_o, kt_o, v_o, bon_o, carry):
    @pl.when(pl.program_id(1) == 0)
    def _():
        carry[...] = sh_ref[...]

    tb = zs_ref.shape[1]
    row = lax.broadcasted_iota(jnp.int32, (tb, 1), 0)
    outs = [[] for _ in range(6)]
    for bq in range(QUAD):
        zs = zs_ref[bq]
        rolled = pltpu.roll(zs, 1, axis=0)
        prev = jnp.where(row == 0, carry[bq], rolled)
        carry[bq] = zs[tb - 1:tb, :]
        xs = zs + mu_ref[...] * (prev - zs)
        r = xs[:, 0:RWKV_W]
        k = xs[:, RWKV_W:2 * RWKV_W]
        v = xs[:, 2 * RWKV_W:3 * RWKV_W]
        wd = xs[:, 3 * RWKV_W:3 * RWKV_W + LORA]
        ad = xs[:, 3 * RWKV_W + LORA:SHIFT_W]
        wl = w0_ref[...] + jnp.dot(jnp.tanh(wd).astype(BF16), wup_ref[...], preferred_element_type=F32)
        dec = jnp.exp(-DECAY_SCALE * _sigmoid(wl))
        a = _sigmoid(a0_ref[...] + jnp.dot(ad.astype(BF16), aup_ref[...], preferred_element_type=F32))
        kkv = k * kk_ref[...]
        norm = jnp.maximum(jnp.sqrt(_head_sum(kkv * kkv, j_ref)), 1e-12)
        kap = kkv / norm
        kt = k * (1.0 + (a - 1.0) * ka_ref[...])
        bon_o[bq] = _head_sum(r * kt * rk_ref[...], j_ref) * v
        for lst, val in zip(outs, (r, dec, kap, kap * a, kt, v)):
            lst.append(val)
    for lst, o_ref in zip(outs, (r_o, w_o, kap_o, b_o, kt_o, v_o)):
        _interleave_store(lst, o_ref)


def _rwkv_prep(zs, shift0, p):
    B, T, _ = zs.shape
    G = B // QUAD
    tb = _tile(T, (32, 16, 8))
    row = lambda n: pl.BlockSpec((1, n), lambda g, i: (0, 0))
    full = lambda a: pl.BlockSpec(a.shape, lambda g, i: (0, 0))
    op = jax.ShapeDtypeStruct((G, T, HEAD * LANES), F32)
    opspec = pl.BlockSpec((1, tb, HEAD * LANES), lambda g, i: (g, i, 0))
    return pl.pallas_call(
        _prep_kernel,
        out_shape=(op,) * 6 + (jax.ShapeDtypeStruct((B, T, RWKV_W), F32),),
        grid=(G, T // tb),
        in_specs=[pl.BlockSpec((QUAD, tb, SHIFT_W), lambda g, i: (g, i, 0)),
                  pl.BlockSpec((QUAD, 1, SHIFT_W), lambda g, i: (g, 0, 0)),
                  row(SHIFT_W), row(RWKV_W), full(p["wup"]), row(RWKV_W), full(p["aup"]),
                  row(RWKV_W), row(RWKV_W), row(RWKV_W), full(p["jmat"])],
        out_specs=(opspec,) * 6 + (pl.BlockSpec((QUAD, tb, RWKV_W), lambda g, i: (g, i, 0)),),
        scratch_shapes=[pltpu.VMEM((QUAD, 1, SHIFT_W), F32)],
        compiler_params=_params("parallel", "arbitrary"),
        name="rwkv_prep",
    )(zs, shift0, p["mu"], p["w0"], p["wup"], p["a0"], p["aup"], p["k_k"], p["k_a"], p["r_k"], p["jmat"])


def _scan_kernel(r_ref, w_ref, kap_ref, b_ref, kt_ref, v_ref, s0_ref, y_ref, sout_ref, S):
    ti = pl.program_id(1)

    @pl.when(ti == 0)
    def _():
        S[...] = s0_ref[...]

    tb = r_ref.shape[1]

    u0 = S[0] * kap_ref[0, 0, 0:1, :]
    for k in range(1, HEAD):
        u0 = u0 + S[k] * kap_ref[0, 0, k:k + 1, :]

    def step(t, u):
        tn = jnp.minimum(t + 1, tb - 1)
        vt = v_ref[0, t]
        y = None
        un = None
        for k in range(HEAD):
            s = S[k] * w_ref[0, t, k:k + 1, :] - u * b_ref[0, t, k:k + 1, :] + vt * kt_ref[0, t, k:k + 1, :]
            S[k] = s
            ty = s * r_ref[0, t, k:k + 1, :]
            tu = s * kap_ref[0, tn, k:k + 1, :]
            y = ty if y is None else y + ty
            un = tu if un is None else un + tu
        m = jnp.mean(y, axis=0, keepdims=True)
        d = y - m
        var = jnp.mean(d * d, axis=0, keepdims=True)
        y_ref[0, t] = d * lax.rsqrt(var + GN_EPS)
        return un

    lax.fori_loop(0, tb, step, u0)

    @pl.when(ti == pl.num_programs(1) - 1)
    def _():
        sout_ref[...] = S[...]


def _wkv_scan(ops, s0):
    G, T = ops[0].shape[:2]
    tb = _tile(T, (32, 16, 8))
    ospec = pl.BlockSpec((1, tb, HEAD, LANES), lambda g, i: (g, i, 0, 0))
    sspec = pl.BlockSpec((HEAD, HEAD, LANES), lambda g, i: (0, 0, g))
    return pl.pallas_call(
        _scan_kernel,
        out_shape=(jax.ShapeDtypeStruct((G, T, HEAD, LANES), F32),
                   jax.ShapeDtypeStruct((HEAD, HEAD, G * LANES), F32)),
        grid=(G, T // tb),
        in_specs=[ospec] * 6 + [sspec],
        out_specs=(ospec, sspec),
        scratch_shapes=[pltpu.VMEM((HEAD, HEAD, LANES), F32)],
        compiler_params=_params("parallel", "arbitrary"),
        name="wkv_scan",
    )(*ops, s0)


def _gate_kernel(yn_ref, bon_ref, g_ref, gw_ref, gb_ref, o_ref):
    for m in range(RWKV_W // LANES):
        tiles = [yn_ref[0, :, (m * QUAD + j) * LANES:(m * QUAD + j + 1) * LANES] for j in range(QUAD)]
        sl = slice(m * LANES, (m + 1) * LANES)
        for bq in range(QUAD):
            yn = _lane_quarter_select([_roll_lanes(t, (j - bq) * RWKV_HEADS) for j, t in enumerate(tiles)])
            y = yn * gw_ref[:, sl] + gb_ref[:, sl] + bon_ref[bq, :, sl]
            o_ref[bq, :, sl] = (y * _silu(g_ref[bq, :, sl].astype(F32))).astype(o_ref.dtype)


def _rwkv_gate(yn, bonus, f, gn_w, gn_b):
    B, T, _ = bonus.shape
    G = B // QUAD
    tb = _tile(T, (128, 64, 32, 16, 8))
    spec = pl.BlockSpec((QUAD, tb, RWKV_W), lambda g_, i: (g_, i, 0))
    row = pl.BlockSpec((1, RWKV_W), lambda g_, i: (0, 0))
    return pl.pallas_call(
        _gate_kernel,
        out_shape=jax.ShapeDtypeStruct((B, T, RWKV_W), BF16),
        grid=(G, T // tb),
        in_specs=[pl.BlockSpec((1, tb, HEAD * LANES), lambda g_, i: (g_, i, 0)), spec,
                  pl.BlockSpec((QUAD, tb, RWKV_W), lambda g_, i: (g_, i, F_GR // RWKV_W)), row, row],
        out_specs=spec,
        compiler_params=_params("parallel", "parallel"),
        name="rwkv_gate",
    )(yn, bonus, f, gn_w, gn_b)


def _rope_tables(pos):
    half = ROT_DIM // 2
    inv = ROPE_THETA ** (-jnp.arange(half, dtype=F32) * 2.0 / ROT_DIM)
    ang = pos.astype(F32)[:, None] * inv[None, :]
    cos, sin = jnp.cos(ang), jnp.sin(ang)
    n = pos.shape[0]
    one = jnp.ones((n, HEAD - ROT_DIM), F32)
    zero = jnp.zeros((n, HEAD - ROT_DIM), F32)
    zh = jnp.zeros((n, half), F32)
    c = jnp.concatenate([cos, cos, one], axis=1)
    s1 = jnp.concatenate([zh, sin, zero], axis=1)
    s2 = jnp.concatenate([-sin, zh, zero], axis=1)
    reps = LANES // HEAD
    return jnp.tile(c, (1, reps)), jnp.tile(s1, (1, reps)), jnp.tile(s2, (1, reps))


def _rope(x, c, s1, s2):
    half = ROT_DIM // 2
    return x * c + pltpu.roll(x, half, axis=1) * s1 + pltpu.roll(x, LANES - half, axis=1) * s2


def _dup_heads(x):
    sw = pltpu.roll(x, HEAD, axis=1)
    low = lax.broadcasted_iota(jnp.int32, x.shape, 1) < HEAD
    return jnp.where(low, x, sw), jnp.where(low, sw, x)


def _softmax_with_sink(s, sink):
    m = jnp.maximum(jnp.max(s, axis=1, keepdims=True), sink)
    e = jnp.exp2(s - m)
    den = jnp.sum(e, axis=1, keepdims=True) + jnp.exp2(sink - m)
    return e * (1.0 / den)


def _pair_attention(q, k2, v2, mask, sink0, sink1):
    n = q.shape[0]
    low = lax.broadcasted_iota(jnp.int32, q.shape, 1) < HEAD
    q = q * QK_SCALE
    qs = jnp.concatenate([jnp.where(low, q, 0.0), jnp.where(low, 0.0, q)], axis=0).astype(BF16)
    s = lax.dot_general(qs, k2, (((1,), (1,)), ((), ())), preferred_element_type=F32)
    s = jnp.where(mask, s, -1e30)
    top = lax.broadcasted_iota(jnp.int32, (2 * n, 1), 0) < n
    sink = jnp.where(top, sink0, sink1) * LOG2E
    p = _softmax_with_sink(s, sink).astype(BF16)
    o = jnp.dot(p, v2, preferred_element_type=F32)
    return jnp.where(low, o[:n], o[n:])


def _attn_prompt_kernel(sink_ref, q_ref, kc_ref, kp_ref, vc_ref, vp_ref, g_ref,
                        cc_ref, s1c_ref, s2c_ref, cp_ref, s1p_ref, s2p_ref, o_ref, krot_ref):
    i = pl.program_id(1)
    W = WINDOW
    qi = lax.broadcasted_iota(jnp.int32, (2 * W, 2 * W), 0) % W
    kj = lax.broadcasted_iota(jnp.int32, (2 * W, 2 * W), 1)
    rel = W + qi - kj
    mask = (rel >= 0) & (rel < W) & ((i > 0) | (kj >= W))
    cc, s1c, s2c = cc_ref[...], s1c_ref[...], s2c_ref[...]
    cp, s1p, s2p = cp_ref[...], s1p_ref[...], s2p_ref[...]
    k2, v2 = [], []
    for t in range(KV_W // LANES):
        sl = slice(t * LANES, (t + 1) * LANES)
        kc = _rope(kc_ref[:, sl], cc, s1c, s2c)
        krot_ref[:, sl] = kc
        kcat = jnp.concatenate([_rope(kp_ref[:, sl], cp, s1p, s2p), kc], axis=0)
        vcat = jnp.concatenate([vp_ref[:, sl], vc_ref[:, sl]], axis=0)
        k2.extend(x.astype(BF16) for x in _dup_heads(kcat))
        v2.extend(x.astype(BF16) for x in _dup_heads(vcat))
    pairs_per_kv = N_Q_HEADS // N_KV_HEADS // 2
    for p in range(N_Q_HEADS // 2):
        sl = slice(p * LANES, (p + 1) * LANES)
        q = _rope(q_ref[:, sl].astype(F32), cc, s1c, s2c)
        g = p // pairs_per_kv
        o = _pair_attention(q, k2[g], v2[g], mask, sink_ref[2 * p], sink_ref[2 * p + 1])
        o_ref[:, sl] = (o * _silu(g_ref[:, sl].astype(F32))).astype(o_ref.dtype)


def _attn_prompt(f, zskv, sinks, B, T):
    W = WINDOW
    nb = T // W
    tabs = _rope_tables(jnp.arange(T))
    kcol = SHIFT_W // KV_W
    cur = lambda b, i: (b * nb + i, 0)
    tcur = pl.BlockSpec((W, LANES), lambda b, i: (i, 0))
    tprv = pl.BlockSpec((W, LANES), lambda b, i: (jnp.maximum(i - 1, 0), 0))
    return pl.pallas_call(
        _attn_prompt_kernel,
        out_shape=(jax.ShapeDtypeStruct((B * T, ATT_W), BF16), jax.ShapeDtypeStruct((B * T, KV_W), F32)),
        grid=(B, nb),
        in_specs=[pl.BlockSpec(memory_space=pltpu.SMEM),
                  pl.BlockSpec((W, ATT_W), lambda b, i: (b * nb + i, F_Q // ATT_W)),
                  pl.BlockSpec((W, KV_W), lambda b, i: (b * nb + i, kcol)),
                  pl.BlockSpec((W, KV_W), lambda b, i: (b * nb + jnp.maximum(i - 1, 0), kcol)),
                  pl.BlockSpec((W, KV_W), lambda b, i: (b * nb + i, kcol + 1)),
                  pl.BlockSpec((W, KV_W), lambda b, i: (b * nb + jnp.maximum(i - 1, 0), kcol + 1)),
                  pl.BlockSpec((W, ATT_W), lambda b, i: (b * nb + i, F_GA // ATT_W)),
                  tcur, tcur, tcur, tprv, tprv, tprv],
        out_specs=(pl.BlockSpec((W, ATT_W), cur), pl.BlockSpec((W, KV_W), cur)),
        compiler_params=_params("parallel", "parallel"),
        name="attn_prompt",
    )(sinks, f, zskv, zskv, zskv, zskv, f, *tabs, *tabs)


def _attn_sample_kernel(q_ref, kn_ref, vn_ref, ck_ref, cv_ref, g_ref, sink_ref, c_ref, s1_ref, s2_ref,
                        o_ref, ko_ref, vo_ref):
    W = WINDOW
    nbat = ck_ref.shape[0]
    Ts = q_ref.shape[0] // nbat
    GQ = N_Q_HEADS // N_KV_HEADS
    GW = GQ * HEAD
    nrow = N_Q_HEADS * Ts
    tq = lax.broadcasted_iota(jnp.int32, (nrow, W + Ts), 0) % Ts
    kj = lax.broadcasted_iota(jnp.int32, (nrow, W + Ts), 1)
    rel = W + tq - kj
    mask = (rel >= 0) & (rel < W)
    sink = sink_ref[...] * LOG2E
    lane_head = lax.broadcasted_iota(jnp.int32, (Ts, GW), 1) // HEAD
    c, s1, s2 = c_ref[...], s1_ref[...], s2_ref[...]
    tiles_per_group = GW // LANES
    for bi in range(nbat):
        rows = slice(bi * Ts, (bi + 1) * Ts)
        k2, v2 = [], []
        for t in range(KV_W // LANES):
            sl = slice(t * LANES, (t + 1) * LANES)
            kcat = jnp.concatenate([ck_ref[bi, :, sl], _rope(kn_ref[rows, sl], c, s1, s2)], axis=0)
            vcat = jnp.concatenate([cv_ref[bi, :, sl], vn_ref[rows, sl]], axis=0)
            ko_ref[bi, :, sl] = kcat[Ts:, :]
            vo_ref[bi, :, sl] = vcat[Ts:, :]
            k2.extend(x.astype(BF16) for x in _dup_heads(kcat))
            v2.extend(x.astype(BF16) for x in _dup_heads(vcat))
        s_parts = []
        for g in range(N_KV_HEADS):
            qg = jnp.concatenate(
                [_rope(q_ref[rows, (g * tiles_per_group + i) * LANES:(g * tiles_per_group + i + 1) * LANES]
                       .astype(F32), c, s1, s2)
                 for i in range(tiles_per_group)], axis=1)
            qg = qg * QK_SCALE
            qs = jnp.concatenate([jnp.where(lane_head == h, qg, 0.0) for h in range(GQ)], axis=0).astype(BF16)
            k8 = jnp.concatenate([k2[g]] * tiles_per_group, axis=1)
            s_parts.append(lax.dot_general(qs, k8, (((1,), (1,)), ((), ())), preferred_element_type=F32))
        s = jnp.where(mask, jnp.concatenate(s_parts, axis=0), -1e30)
        p = _softmax_with_sink(s, sink).astype(BF16)
        for g in range(N_KV_HEADS):
            v8 = jnp.concatenate([v2[g]] * tiles_per_group, axis=1)
            o = jnp.dot(p[g * GQ * Ts:(g + 1) * GQ * Ts], v8, preferred_element_type=F32)
            og = o[0:Ts]
            for h in range(1, GQ):
                og = jnp.where(lane_head == h, o[h * Ts:(h + 1) * Ts], og)
            gl = slice(g * GW, (g + 1) * GW)
            o_ref[rows, gl] = (og * _silu(g_ref[rows, gl].astype(F32))).astype(o_ref.dtype)


def _attn_sample(f, zskv, sinks, cache_k, cache_v, B, Ts):
    W = WINDOW
    nbat = _tile(B, (4, 2, 1))
    tabs = _rope_tables(PAST_LEN + jnp.arange(Ts))
    kcol = SHIFT_W // KV_W
    rows = nbat * Ts
    sink_col = jnp.repeat(sinks, Ts).reshape(N_Q_HEADS * Ts, 1)
    cspec = pl.BlockSpec((nbat, W, KV_W), lambda i: (i, 0, 0))
    tspec = pl.BlockSpec((Ts, LANES), lambda i: (0, 0))
    cache = jax.ShapeDtypeStruct((B, W, KV_W), F32)
    return pl.pallas_call(
        _attn_sample_kernel,
        out_shape=(jax.ShapeDtypeStruct((B * Ts, ATT_W), BF16), cache, cache),
        grid=(B // nbat,),
        in_specs=[pl.BlockSpec((rows, ATT_W), lambda i: (i, F_Q // ATT_W)),
                  pl.BlockSpec((rows, KV_W), lambda i: (i, kcol)),
                  pl.BlockSpec((rows, KV_W), lambda i: (i, kcol + 1)),
                  cspec, cspec,
                  pl.BlockSpec((rows, ATT_W), lambda i: (i, F_GA // ATT_W)),
                  pl.BlockSpec((N_Q_HEADS * Ts, 1), lambda i: (0, 0)),
                  tspec, tspec, tspec],
        out_specs=(pl.BlockSpec((rows, ATT_W), lambda i: (i, 0)), cspec, cspec),
        compiler_params=_params("parallel"),
        name="attn_sample",
    )(f, zskv, zskv, cache_k, cache_v, f, sink_col, *tabs)


def _mix_kernel(yr_ref, ya_ref, wr_ref, wa_ref, gr_ref, ga_ref, o_ref):
    br = jnp.dot(yr_ref[...], wr_ref[...], preferred_element_type=F32)
    ba = jnp.dot(ya_ref[...], wa_ref[...], preferred_element_type=F32)
    o_ref[...] = (_sigmoid(gr_ref[...].astype(F32)) * br
                  + _sigmoid(ga_ref[...].astype(F32)) * ba).astype(o_ref.dtype)


def _mix(y_r, y_a, w_r, w_a, f):
    M = y_r.shape[0]
    D = w_r.shape[1]
    tm = _tile(M, (1024, 512, 256, 128))
    tn = _tile(D, (512, 256, 128))
    nj = D // tn
    j0 = F_M // tn
    return pl.pallas_call(
        _mix_kernel,
        out_shape=jax.ShapeDtypeStruct((M, D), BF16),
        grid=(M // tm, nj),
        in_specs=[pl.BlockSpec((tm, RWKV_W), lambda i, j: (i, 0)),
                  pl.BlockSpec((tm, ATT_W), lambda i, j: (i, 0)),
                  pl.BlockSpec((RWKV_W, tn), lambda i, j: (0, j)),
                  pl.BlockSpec((ATT_W, tn), lambda i, j: (0, j)),
                  pl.BlockSpec((tm, tn), lambda i, j: (i, j0 + j)),
                  pl.BlockSpec((tm, tn), lambda i, j: (i, j0 + nj + j))],
        out_specs=pl.BlockSpec((tm, tn), lambda i, j: (i, j)),
        compiler_params=_params("parallel", "arbitrary"),
        name="branch_mix",
    )(y_r, y_a, w_r, w_a, f, f)


def _out_kernel(mix_ref, w_ref, x_ref, g_ref, b_ref, o_ref, *, tn):
    j = pl.program_id(1)
    col = pl.multiple_of(j * tn, tn)
    out = jnp.dot(mix_ref[...], w_ref[...], preferred_element_type=F32)
    o_ref[:, pl.ds(col, tn)] = ALPHA * x_ref[...] + out

    @pl.when(j == pl.num_programs(1) - 1)
    def _():
        h = o_ref[...]
        m = jnp.mean(h, axis=1, keepdims=True)
        d = h - m
        var = jnp.mean(d * d, axis=1, keepdims=True)
        o_ref[...] = d * lax.rsqrt(var + LN_EPS) * g_ref[...] + b_ref[...]


def _out_ln(mix, w_out, x, ln_g, ln_b):
    M, D = x.shape
    tm = _tile(M, (512, 256, 128))
    tn = _tile(D, (512, 256, 128))
    row = pl.BlockSpec((1, D), lambda i, j: (0, 0))
    return pl.pallas_call(
        functools.partial(_out_kernel, tn=tn),
        out_shape=jax.ShapeDtypeStruct((M, D), F32),
        grid=(M // tm, D // tn),
        in_specs=[pl.BlockSpec((tm, D), lambda i, j: (i, 0)),
                  pl.BlockSpec((D, tn), lambda i, j: (0, j)),
                  pl.BlockSpec((tm, tn), lambda i, j: (i, j)),
                  row, row],
        out_specs=pl.BlockSpec((tm, D), lambda i, j: (i, 0)),
        compiler_params=_params("parallel", "arbitrary"),
        name="out_ln",
    )(mix, w_out, x, ln_g, ln_b)


def _k_major(x, axis):
    shp = x.shape
    n = shp[axis] // RWKV_W
    y = x.reshape(shp[:axis] + (n, RWKV_HEADS, HEAD) + shp[axis + 1:])
    y = jnp.swapaxes(y, axis + 1, axis + 2)
    return y.reshape(shp)


def _head_major(x, axis):
    shp = x.shape
    n = shp[axis] // RWKV_W
    y = x.reshape(shp[:axis] + (n, HEAD, RWKV_HEADS) + shp[axis + 1:])
    y = jnp.swapaxes(y, axis + 1, axis + 2)
    return y.reshape(shp)


def _shift_k_major(x):
    return jnp.concatenate([_k_major(x[..., :3 * RWKV_W], x.ndim - 1), x[..., 3 * RWKV_W:]], axis=-1)


def _shift_head_major(x):
    return jnp.concatenate([_head_major(x[..., :3 * RWKV_W], x.ndim - 1), x[..., 3 * RWKV_W:]], axis=-1)


def _layer(x, S0, shift0, caches, w, p):
    B, T, D = x.shape
    M = B * T
    G = B // QUAD
    x2 = x.reshape(M, D)
    xb = x2.astype(BF16)
    zskv = _matmul(xb, w["zskv"], F32, name="proj_shift_kv")
    zskv3 = zskv.reshape(B, T, ZSKV_W)
    prep = _rwkv_prep(zskv3, _shift_k_major(shift0).reshape(B, 1, SHIFT_W), p)
    ops = [a.reshape(G, T, HEAD, LANES) for a in prep[:6]]
    s0 = S0.transpose(3, 2, 0, 1).reshape(HEAD, HEAD, B * RWKV_HEADS)
    f = _matmul(xb, w["f"], BF16, name="proj_gates_q")
    yn, s_fin = _wkv_scan(ops, s0)
    y_r = _rwkv_gate(yn.reshape(G, T, HEAD * LANES), prep[6], f.reshape(B, T, -1),
                     p["gn_w"], p["gn_b"]).reshape(M, RWKV_W)
    S_new = s_fin.reshape(HEAD, HEAD, B, RWKV_HEADS).transpose(2, 3, 1, 0)
    shift_new = _shift_head_major(zskv3[:, T - 1, :SHIFT_W])

    if caches is None:
        y_a, k_rot = _attn_prompt(f, zskv, p["sinks"], B, T)
        k_buf = k_rot.reshape(B, T, N_KV_HEADS, HEAD)[:, T - WINDOW:]
        v_buf = zskv3[:, T - WINDOW:, SHIFT_W + KV_W:].reshape(B, WINDOW, N_KV_HEADS, HEAD)
    else:
        y_a, k_buf, v_buf = _attn_sample(f, zskv, p["sinks"], caches[0], caches[1], B, T)
        k_buf = k_buf.reshape(B, WINDOW, N_KV_HEADS, HEAD)
        v_buf = v_buf.reshape(B, WINDOW, N_KV_HEADS, HEAD)

    mix = _mix(y_r, y_a, w["branch_r"], w["branch_a"], f)
    y = _out_ln(mix, w["out"], x2, p["ln_g"], p["ln_b"]).reshape(B, T, D)
    return y, S_new, shift_new, k_buf, v_buf


def kernel(x_prompt, x_sample, state_wkv, state_shift, cache_win_k, cache_win_v, w_in, mu_shift, w0, w_decay_up, a0, w_aaa_up, k_k, k_a, r_k, gn_w, gn_b, sinks, w_branch_rwkv, w_branch_attn, w_out, ln_g, ln_b):
    assert w_in.shape[0] == DEPTH
    Bp = x_prompt.shape[0]
    Bs = x_sample.shape[0]
    assert Bp % QUAD == 0 and Bs % QUAD == 0
    o_gr = SHIFT_W
    o_q = o_gr + RWKV_W
    o_k = o_q + ATT_W
    o_ga = o_k + 2 * KV_W
    o_m = o_ga + ATT_W
    wi = w_in[0]
    w = {"zskv": jnp.concatenate([_shift_k_major(wi[:, :o_gr]), wi[:, o_k:o_ga]], axis=1).astype(BF16),
         "f": jnp.concatenate([wi[:, o_ga:o_m], _k_major(wi[:, o_gr:o_q], 1), wi[:, o_q:o_k], wi[:, o_m:]],
                              axis=1).astype(BF16),
         "branch_r": _k_major(w_branch_rwkv[0], 0).astype(BF16),
         "branch_a": w_branch_attn[0].astype(BF16), "out": w_out[0].astype(BF16)}
    lane = jnp.arange(LANES)
    km = lambda a: _k_major(a.reshape(1, RWKV_W), 1)
    p = {"mu": _shift_k_major(mu_shift), "w0": km(w0), "wup": _k_major(w_decay_up[0], 1).astype(BF16),
         "a0": km(a0), "aup": _k_major(w_aaa_up[0], 1).astype(BF16), "k_k": km(k_k), "k_a": km(k_a),
         "r_k": km(r_k), "gn_w": km(gn_w), "gn_b": km(gn_b), "sinks": sinks[0], "ln_g": ln_g, "ln_b": ln_b,
         "jmat": (lane[:, None] % RWKV_HEADS == lane[None, :] % RWKV_HEADS).astype(BF16)}

    S0p = jnp.zeros((Bp, RWKV_HEADS, HEAD, HEAD), F32)
    sh0p = jnp.zeros((Bp, SHIFT_W), F32)
    yp, a1, a2, a3, a4 = _layer(x_prompt, S0p, sh0p, None, w, p)
    caches = (cache_win_k[0].reshape(Bs, WINDOW, KV_W), cache_win_v[0].reshape(Bs, WINDOW, KV_W))
    ys, b1, b2, b3, b4 = _layer(x_sample, state_wkv[0], state_shift[0], caches, w, p)
    return (yp, ys, a1[None], a2[None], a3[None], a4[None], b1[None], b2[None], b3[None], b4[None])
```

```python
import functools
import math

import jax
import jax.numpy as jnp
from jax import lax
from jax.experimental import pallas as pl
from jax.experimental.pallas import tpu as pltpu

RWKV_HEADS = 32
HEAD = 64
RWKV_W = RWKV_HEADS * HEAD
LORA = 128
SHIFT_W = 3 * RWKV_W + 2 * LORA
GN_EPS = 64e-5
N_Q_HEADS = 32
N_KV_HEADS = 4
ATT_W = N_Q_HEADS * HEAD
KV_W = N_KV_HEADS * HEAD
WINDOW = 128
ROT_DIM = HEAD // 4
ROPE_THETA = 500000.0
PAST_LEN = 8192
DEPTH = 1
ALPHA = (2.0 * DEPTH) ** 0.25
LN_EPS = 1e-5
LANES = 128
QUAD = LANES // RWKV_HEADS
DECAY_SCALE = math.exp(-0.5)
LOG2E = math.log2(math.e)
QK_SCALE = LOG2E / math.sqrt(HEAD)
ZSKV_W = SHIFT_W + 2 * KV_W
W_Q = SHIFT_W + RWKV_W
F_Q = 0
F_GA = ATT_W + 2 * KV_W
F_M = F_GA + ATT_W
F_GA_BLOCK = F_GA
W_TILE = 256
VMEM_LIMIT = 56 * 1024 * 1024

BF16 = jnp.bfloat16
F32 = jnp.float32


def _params(*sem):
    return pltpu.CompilerParams(dimension_semantics=sem, vmem_limit_bytes=VMEM_LIMIT)


def _tile(n, prefs):
    for p in prefs:
        if n % p == 0:
            return p
    return n


def _sigmoid(x):
    return 1.0 / (1.0 + jnp.exp(-x))


def _silu(x):
    return x * _sigmoid(x)


def _mm_kernel(x_ref, w_ref, o_ref):
    o_ref[...] = jnp.dot(x_ref[...], w_ref[...], preferred_element_type=F32).astype(o_ref.dtype)


def _matmul(x, w, out_dtype, name="proj"):
    M, K = x.shape
    N = w.shape[1]
    tm = _tile(M, (1024, 512, 256, 128))
    tn = _tile(N, (512, 768, 640, 256, 128))
    return pl.pallas_call(
        _mm_kernel,
        out_shape=jax.ShapeDtypeStruct((M, N), out_dtype),
        grid=(M // tm, N // tn),
        in_specs=[pl.BlockSpec((tm, K), lambda i, j: (i, 0)),
                  pl.BlockSpec((K, tn), lambda i, j: (0, j))],
        out_specs=pl.BlockSpec((tm, tn), lambda i, j: (i, j)),
        compiler_params=_params("parallel", "arbitrary"),
        name=name,
    )(x, w)


def _mm2_kernel(x_ref, wa_ref, wb_ref, o_ref):
    x = x_ref[...]
    o_ref[:, :W_TILE] = jnp.dot(x, wa_ref[...], preferred_element_type=F32).astype(o_ref.dtype)
    o_ref[:, W_TILE:] = jnp.dot(x, wb_ref[...], preferred_element_type=F32).astype(o_ref.dtype)


def _matmul_cols(x, w, col0, out_dtype, name="proj"):
    M, K = x.shape
    N = w.shape[1] - col0
    assert col0 % W_TILE == 0 and N % (2 * W_TILE) == 0
    tm = _tile(M, (1024, 512, 256, 128))
    j0 = col0 // W_TILE
    return pl.pallas_call(
        _mm2_kernel,
        out_shape=jax.ShapeDtypeStruct((M, N), out_dtype),
        grid=(M // tm, N // (2 * W_TILE)),
        in_specs=[pl.BlockSpec((tm, K), lambda i, j: (i, 0)),
                  pl.BlockSpec((K, W_TILE), lambda i, j: (0, j0 + 2 * j)),
                  pl.BlockSpec((K, W_TILE), lambda i, j: (0, j0 + 2 * j + 1))],
        out_specs=pl.BlockSpec((tm, 2 * W_TILE), lambda i, j: (i, j)),
        compiler_params=_params("parallel", "arbitrary"),
        name=name,
    )(x, w, w)


def _head_sum(x, j_ref):
    nt = x.shape[1] // LANES
    t = x[:, 0:LANES]
    for m in range(1, nt):
        t = t + x[:, m * LANES:(m + 1) * LANES]
    hi = t.astype(BF16)
    lo = (t - hi.astype(F32)).astype(BF16)
    s = (jnp.dot(hi, j_ref[...], preferred_element_type=F32)
         + jnp.dot(lo, j_ref[...], preferred_element_type=F32))
    return jnp.concatenate([s] * nt, axis=1)


def _lane_quarter_select(parts):
    lane = lax.broadcasted_iota(jnp.int32, parts[0].shape, 1)
    out = parts[QUAD - 1]
    for q in range(QUAD - 2, -1, -1):
        out = jnp.where(lane < (q + 1) * RWKV_HEADS, parts[q], out)
    return out


def _roll_lanes(x, shift):
    shift = shift % LANES
    return x if shift == 0 else pltpu.roll(x, shift, axis=1)


def _interleave_store(xs, o_ref):
    for m in range(RWKV_W // LANES):
        tiles = [x[:, m * LANES:(m + 1) * LANES] for x in xs]
        for j in range(QUAD):
            parts = [_roll_lanes(t, (b - j) * RWKV_HEADS) for b, t in enumerate(tiles)]
            k = m * QUAD + j
            o_ref[0, :, k * LANES:(k + 1) * LANES] = _lane_quarter_select(parts)


def _prep_kernel(zs_ref, sh_ref, mu_ref, w0_ref, wup_ref, a0_ref, aup_ref, kk_ref, ka_ref, rk_ref, j_ref,
                 r_o, w_o, kap_o, b_o, kt_o, v_o, bon_o, carry):
    @pl.when(pl.program_id(1) == 0)
    def _():
        carry[...] = sh_ref[...]

    tb = zs_ref.shape[1]
    row = lax.broadcasted_iota(jnp.int32, (tb, 1), 0)
    outs = [[] for _ in range(6)]
    for bq in range(QUAD):
        zs = zs_ref[bq]
        rolled = pltpu.roll(zs, 1, axis=0)
        prev = jnp.where(row == 0, carry[bq], rolled)
        carry[bq] = zs[tb - 1:tb, :]
        xs = zs + mu_ref[...] * (prev - zs)
        r = xs[:, 0:RWKV_W]
        k = xs[:, RWKV_W:2 * RWKV_W]
        v = xs[:, 2 * RWKV_W:3 * RWKV_W]
        wd = xs[:, 3 * RWKV_W:3 * RWKV_W + LORA]
        ad = xs[:, 3 * RWKV_W + LORA:SHIFT_W]
        wl = w0_ref[...] + jnp.dot(jnp.tanh(wd).astype(BF16), wup_ref[...], preferred_element_type=F32)
        dec = jnp.exp(-DECAY_SCALE * _sigmoid(wl))
        a = _sigmoid(a0_ref[...] + jnp.dot(ad.astype(BF16), aup_ref[...], preferred_element_type=F32))
        kkv = k * kk_ref[...]
        norm = jnp.maximum(jnp.sqrt(_head_sum(kkv * kkv, j_ref)), 1e-12)
        kap = kkv / norm
        kt = k * (1.0 + (a - 1.0) * ka_ref[...])
        bon_o[bq] = _head_sum(r * kt * rk_ref[...], j_ref) * v
        for lst, val in zip(outs, (r, dec, kap, kap * a, kt, v)):
            lst.append(val)
    for lst, o_ref in zip(outs, (r_o, w_o, kap_o, b_o, kt_o, v_o)):
        _interleave_store(lst, o_ref)


def _rwkv_prep(zs, shift0, p):
    B, T, _ = zs.shape
    G = B // QUAD
    tb = _tile(T, (32, 16, 8))
    row = lambda n: pl.BlockSpec((1, n), lambda g, i: (0, 0))
    full = lambda a: pl.BlockSpec(a.shape, lambda g, i: (0, 0))
    op = jax.ShapeDtypeStruct((G, T, HEAD * LANES), F32)
    opspec = pl.BlockSpec((1, tb, HEAD * LANES), lambda g, i: (g, i, 0))
    return pl.pallas_call(
        _prep_kernel,
        out_shape=(op,) * 6 + (jax.ShapeDtypeStruct((B, T, RWKV_W), F32),),
        grid=(G, T // tb),
        in_specs=[pl.BlockSpec((QUAD, tb, SHIFT_W), lambda g, i: (g, i, 0)),
                  pl.BlockSpec((QUAD, 1, SHIFT_W), lambda g, i: (g, 0, 0)),
                  row(SHIFT_W), row(RWKV_W), full(p["wup"]), row(RWKV_W), full(p["aup"]),
                  row(RWKV_W), row(RWKV_W), row(RWKV_W), full(p["jmat"])],
        out_specs=(opspec,) * 6 + (pl.BlockSpec((QUAD, tb, RWKV_W), lambda g, i: (g, i, 0)),),
        scratch_shapes=[pltpu.VMEM((QUAD, 1, SHIFT_W), F32)],
        compiler_params=_params("parallel", "arbitrary"),
        name="rwkv_prep",
    )(zs, shift0, p["mu"], p["w0"], p["wup"], p["a0"], p["aup"], p["k_k"], p["k_a"], p["r_k"], p["jmat"])


def _scan_kernel(r_ref, w_ref, kap_ref, b_ref, kt_ref, v_ref, s0_ref, y_ref, sout_ref, S):
    ti = pl.program_id(1)

    @pl.when(ti == 0)
    def _():
        S[...] = s0_ref[...]

    tb = r_ref.shape[1]

    u0 = S[0] * kap_ref[0, 0, 0:1, :]
    for k in range(1, HEAD):
        u0 = u0 + S[k] * kap_ref[0, 0, k:k + 1, :]

    def step(t, u):
        tn = jnp.minimum(t + 1, tb - 1)
        vt = v_ref[0, t]
        y = None
        un = None
        for k in range(HEAD):
            s = S[k] * w_ref[0, t, k:k + 1, :] - u * b_ref[0, t, k:k + 1, :] + vt * kt_ref[0, t, k:k + 1, :]
            S[k] = s
            ty = s * r_ref[0, t, k:k + 1, :]
            tu = s * kap_ref[0, tn, k:k + 1, :]
            y = ty if y is None else y + ty
            un = tu if un is None else un + tu
        m = jnp.mean(y, axis=0, keepdims=True)
        d = y - m
        var = jnp.mean(d * d, axis=0, keepdims=True)
        y_ref[0, t] = d * lax.rsqrt(var + GN_EPS)
        return un

    lax.fori_loop(0, tb, step, u0)

    @pl.when(ti == pl.num_programs(1) - 1)
    def _():
        sout_ref[...] = S[...]


def _wkv_scan(ops, s0):
    G, T = ops[0].shape[:2]
    tb = _tile(T, (32, 16, 8))
    ospec = pl.BlockSpec((1, tb, HEAD, LANES), lambda g, i: (g, i, 0, 0))
    sspec = pl.BlockSpec((HEAD, HEAD, LANES), lambda g, i: (0, 0, g))
    return pl.pallas_call(
        _scan_kernel,
        out_shape=(jax.ShapeDtypeStruct((G, T, HEAD, LANES), F32),
                   jax.ShapeDtypeStruct((HEAD, HEAD, G * LANES), F32)),
        grid=(G, T // tb),
        in_specs=[ospec] * 6 + [sspec],
        out_specs=(ospec, sspec),
        scratch_shapes=[pltpu.VMEM((HEAD, HEAD, LANES), F32)],
        compiler_params=_params("parallel", "arbitrary"),
        name="wkv_scan",
    )(*ops, s0)


def _gate_kernel(yn_ref, bon_ref, g_ref, gw_ref, gb_ref, o_ref):
    for m in range(RWKV_W // LANES):
        tiles = [yn_ref[0, :, m * QUAD + j, :] for j in range(QUAD)]
        sl = slice(m * LANES, (m + 1) * LANES)
        for bq in range(QUAD):
            yn = _lane_quarter_select([_roll_lanes(t, (j - bq) * RWKV_HEADS) for j, t in enumerate(tiles)])
            y = yn * gw_ref[:, sl] + gb_ref[:, sl] + bon_ref[bq, :, sl]
            o_ref[bq, :, sl] = (y * _silu(g_ref[bq, :, sl].astype(F32))).astype(o_ref.dtype)


def _rwkv_gate(yn, bonus, g, gn_w, gn_b):
    B, T, _ = bonus.shape
    G = B // QUAD
    tb = _tile(T, (128, 64, 32, 16, 8))
    spec = pl.BlockSpec((QUAD, tb, RWKV_W), lambda g_, i: (g_, i, 0))
    row = pl.BlockSpec((1, RWKV_W), lambda g_, i: (0, 0))
    return pl.pallas_call(
        _gate_kernel,
        out_shape=jax.ShapeDtypeStruct((B, T, RWKV_W), BF16),
        grid=(G, T // tb),
        in_specs=[pl.BlockSpec((1, tb, HEAD, LANES), lambda g_, i: (g_, i, 0, 0)), spec, spec, row, row],
        out_specs=spec,
        compiler_params=_params("parallel", "parallel"),
        name="rwkv_gate",
    )(yn, bonus, g, gn_w, gn_b)


def _rope_tables(pos):
    half = ROT_DIM // 2
    inv = ROPE_THETA ** (-jnp.arange(half, dtype=F32) * 2.0 / ROT_DIM)
    ang = pos.astype(F32)[:, None] * inv[None, :]
    cos, sin = jnp.cos(ang), jnp.sin(ang)
    n = pos.shape[0]
    one = jnp.ones((n, HEAD - ROT_DIM), F32)
    zero = jnp.zeros((n, HEAD - ROT_DIM), F32)
    zh = jnp.zeros((n, half), F32)
    c = jnp.concatenate([cos, cos, one], axis=1)
    s1 = jnp.concatenate([zh, sin, zero], axis=1)
    s2 = jnp.concatenate([-sin, zh, zero], axis=1)
    reps = LANES // HEAD
    return jnp.tile(c, (1, reps)), jnp.tile(s1, (1, reps)), jnp.tile(s2, (1, reps))


def _rope(x, c, s1, s2):
    half = ROT_DIM // 2
    return x * c + pltpu.roll(x, half, axis=1) * s1 + pltpu.roll(x, LANES - half, axis=1) * s2


def _dup_heads(x):
    sw = pltpu.roll(x, HEAD, axis=1)
    low = lax.broadcasted_iota(jnp.int32, x.shape, 1) < HEAD
    return jnp.where(low, x, sw), jnp.where(low, sw, x)


def _softmax_with_sink(s, sink):
    m = jnp.maximum(jnp.max(s, axis=1, keepdims=True), sink)
    e = jnp.exp2(s - m)
    den = jnp.sum(e, axis=1, keepdims=True) + jnp.exp2(sink - m)
    return e * (1.0 / den)


def _pair_attention(q, k2, v2, mask, sink0, sink1):
    n = q.shape[0]
    low = lax.broadcasted_iota(jnp.int32, q.shape, 1) < HEAD
    q = q * QK_SCALE
    qs = jnp.concatenate([jnp.where(low, q, 0.0), jnp.where(low, 0.0, q)], axis=0).astype(BF16)
    s = lax.dot_general(qs, k2, (((1,), (1,)), ((), ())), preferred_element_type=F32)
    s = jnp.where(mask, s, -1e30)
    top = lax.broadcasted_iota(jnp.int32, (2 * n, 1), 0) < n
    sink = jnp.where(top, sink0, sink1) * LOG2E
    p = _softmax_with_sink(s, sink).astype(BF16)
    o = jnp.dot(p, v2, preferred_element_type=F32)
    return jnp.where(low, o[:n], o[n:])


def _attn_prompt_kernel(sink_ref, q_ref, kc_ref, kp_ref, vc_ref, vp_ref, g_ref,
                        cc_ref, s1c_ref, s2c_ref, cp_ref, s1p_ref, s2p_ref, o_ref, krot_ref):
    i = pl.program_id(1)
    W = WINDOW
    qi = lax.broadcasted_iota(jnp.int32, (2 * W, 2 * W), 0) % W
    kj = lax.broadcasted_iota(jnp.int32, (2 * W, 2 * W), 1)
    rel = W + qi - kj
    mask = (rel >= 0) & (rel < W) & ((i > 0) | (kj >= W))
    cc, s1c, s2c = cc_ref[...], s1c_ref[...], s2c_ref[...]
    cp, s1p, s2p = cp_ref[...], s1p_ref[...], s2p_ref[...]
    k2, v2 = [], []
    for t in range(KV_W // LANES):
        sl = slice(t * LANES, (t + 1) * LANES)
        kc = _rope(kc_ref[:, sl], cc, s1c, s2c)
        krot_ref[:, sl] = kc
        kcat = jnp.concatenate([_rope(kp_ref[:, sl], cp, s1p, s2p), kc], axis=0)
        vcat = jnp.concatenate([vp_ref[:, sl], vc_ref[:, sl]], axis=0)
        k2.extend(x.astype(BF16) for x in _dup_heads(kcat))
        v2.extend(x.astype(BF16) for x in _dup_heads(vcat))
    pairs_per_kv = N_Q_HEADS // N_KV_HEADS // 2
    for p in range(N_Q_HEADS // 2):
        sl = slice(p * LANES, (p + 1) * LANES)
        q = _rope(q_ref[:, sl].astype(F32), cc, s1c, s2c)
        g = p // pairs_per_kv
        o = _pair_attention(q, k2[g], v2[g], mask, sink_ref[2 * p], sink_ref[2 * p + 1])
        o_ref[:, sl] = (o * _silu(g_ref[:, sl].astype(F32))).astype(o_ref.dtype)


def _attn_prompt(f, zskv, sinks, B, T):
    W = WINDOW
    nb = T // W
    tabs = _rope_tables(jnp.arange(T))
    kcol = SHIFT_W // KV_W
    cur = lambda b, i: (b * nb + i, 0)
    tcur = pl.BlockSpec((W, LANES), lambda b, i: (i, 0))
    tprv = pl.BlockSpec((W, LANES), lambda b, i: (jnp.maximum(i - 1, 0), 0))
    return pl.pallas_call(
        _attn_prompt_kernel,
        out_shape=(jax.ShapeDtypeStruct((B * T, ATT_W), BF16), jax.ShapeDtypeStruct((B * T, KV_W), F32)),
        grid=(B, nb),
        in_specs=[pl.BlockSpec(memory_space=pltpu.SMEM),
                  pl.BlockSpec((W, ATT_W), lambda b, i: (b * nb + i, F_Q // ATT_W)),
                  pl.BlockSpec((W, KV_W), lambda b, i: (b * nb + i, kcol)),
                  pl.BlockSpec((W, KV_W), lambda b, i: (b * nb + jnp.maximum(i - 1, 0), kcol)),
                  pl.BlockSpec((W, KV_W), lambda b, i: (b * nb + i, kcol + 1)),
                  pl.BlockSpec((W, KV_W), lambda b, i: (b * nb + jnp.maximum(i - 1, 0), kcol + 1)),
                  pl.BlockSpec((W, F_GA_BLOCK), lambda b, i: (b * nb + i, F_GA // F_GA_BLOCK)),
                  tcur, tcur, tcur, tprv, tprv, tprv],
        out_specs=(pl.BlockSpec((W, ATT_W), cur), pl.BlockSpec((W, KV_W), cur)),
        compiler_params=_params("parallel", "parallel"),
        name="attn_prompt",
    )(sinks, f, zskv, zskv, zskv, zskv, f, *tabs, *tabs)


def _attn_sample_kernel(q_ref, kn_ref, vn_ref, ck_ref, cv_ref, g_ref, sink_ref, c_ref, s1_ref, s2_ref,
                        o_ref, ko_ref, vo_ref):
    W = WINDOW
    nbat = ck_ref.shape[0]
    Ts = q_ref.shape[0] // nbat
    GQ = N_Q_HEADS // N_KV_HEADS
    GW = GQ * HEAD
    nrow = N_Q_HEADS * Ts
    tq = lax.broadcasted_iota(jnp.int32, (nrow, W + Ts), 0) % Ts
    kj = lax.broadcasted_iota(jnp.int32, (nrow, W + Ts), 1)
    rel = W + tq - kj
    mask = (rel >= 0) & (rel < W)
    sink = sink_ref[...] * LOG2E
    lane_head = lax.broadcasted_iota(jnp.int32, (Ts, GW), 1) // HEAD
    c, s1, s2 = c_ref[...], s1_ref[...], s2_ref[...]
    tiles_per_group = GW // LANES
    for bi in range(nbat):
        rows = slice(bi * Ts, (bi + 1) * Ts)
        k2, v2 = [], []
        for t in range(KV_W // LANES):
            sl = slice(t * LANES, (t + 1) * LANES)
            kcat = jnp.concatenate([ck_ref[bi, :, sl], _rope(kn_ref[rows, sl], c, s1, s2)], axis=0)
            vcat = jnp.concatenate([cv_ref[bi, :, sl], vn_ref[rows, sl]], axis=0)
            ko_ref[bi, :, sl] = kcat[Ts:, :]
            vo_ref[bi, :, sl] = vcat[Ts:, :]
            k2.extend(x.astype(BF16) for x in _dup_heads(kcat))
            v2.extend(x.astype(BF16) for x in _dup_heads(vcat))
        s_parts = []
        for g in range(N_KV_HEADS):
            qg = jnp.concatenate(
                [_rope(q_ref[rows, (g * tiles_per_group + i) * LANES:(g * tiles_per_group + i + 1) * LANES]
                       .astype(F32), c, s1, s2)
                 for i in range(tiles_per_group)], axis=1)
            qg = qg * QK_SCALE
            qs = jnp.concatenate([jnp.where(lane_head == h, qg, 0.0) for h in range(GQ)], axis=0).astype(BF16)
            k8 = jnp.concatenate([k2[g]] * tiles_per_group, axis=1)
            s_parts.append(lax.dot_general(qs, k8, (((1,), (1,)), ((), ())), preferred_element_type=F32))
        s = jnp.where(mask, jnp.concatenate(s_parts, axis=0), -1e30)
        p = _softmax_with_sink(s, sink).astype(BF16)
        for g in range(N_KV_HEADS):
            v8 = jnp.concatenate([v2[g]] * tiles_per_group, axis=1)
            o = jnp.dot(p[g * GQ * Ts:(g + 1) * GQ * Ts], v8, preferred_element_type=F32)
            og = o[0:Ts]
            for h in range(1, GQ):
                og = jnp.where(lane_head == h, o[h * Ts:(h + 1) * Ts], og)
            gl = slice(g * GW, (g + 1) * GW)
            o_ref[rows, gl] = (og * _silu(g_ref[rows, gl].astype(F32))).astype(o_ref.dtype)


def _attn_sample(f, zskv, sinks, cache_k, cache_v, B, Ts):
    W = WINDOW
    nbat = _tile(B, (4, 2, 1))
    tabs = _rope_tables(PAST_LEN + jnp.arange(Ts))
    kcol = SHIFT_W // KV_W
    rows = nbat * Ts
    sink_col = jnp.repeat(sinks, Ts).reshape(N_Q_HEADS * Ts, 1)
    cspec = pl.BlockSpec((nbat, W, KV_W), lambda i: (i, 0, 0))
    tspec = pl.BlockSpec((Ts, LANES), lambda i: (0, 0))
    cache = jax.ShapeDtypeStruct((B, W, KV_W), F32)
    return pl.pallas_call(
        _attn_sample_kernel,
        out_shape=(jax.ShapeDtypeStruct((B * Ts, ATT_W), BF16), cache, cache),
        grid=(B // nbat,),
        in_specs=[pl.BlockSpec((rows, ATT_W), lambda i: (i, F_Q // ATT_W)),
                  pl.BlockSpec((rows, KV_W), lambda i: (i, kcol)),
                  pl.BlockSpec((rows, KV_W), lambda i: (i, kcol + 1)),
                  cspec, cspec,
                  pl.BlockSpec((rows, F_GA_BLOCK), lambda i: (i, F_GA // F_GA_BLOCK)),
                  pl.BlockSpec((N_Q_HEADS * Ts, 1), lambda i: (0, 0)),
                  tspec, tspec, tspec],
        out_specs=(pl.BlockSpec((rows, ATT_W), lambda i: (i, 0)), cspec, cspec),
        compiler_params=_params("parallel"),
        name="attn_sample",
    )(f, zskv, zskv, cache_k, cache_v, f, sink_col, *tabs)


def _mix_kernel(yr_ref, ya_ref, wr_ref, wa_ref, gr_ref, ga_ref, o_ref):
    br = jnp.dot(yr_ref[...], wr_ref[...], preferred_element_type=F32)
    ba = jnp.dot(ya_ref[...], wa_ref[...], preferred_element_type=F32)
    o_ref[...] = (_sigmoid(gr_ref[...].astype(F32)) * br
                  + _sigmoid(ga_ref[...].astype(F32)) * ba).astype(o_ref.dtype)


def _mix(y_r, y_a, w_r, w_a, f):
    M = y_r.shape[0]
    D = w_r.shape[1]
    tm = _tile(M, (1024, 512, 256, 128))
    tn = _tile(D, (512, 256, 128))
    nj = D // tn
    j0 = F_M // tn
    return pl.pallas_call(
        _mix_kernel,
        out_shape=jax.ShapeDtypeStruct((M, D), BF16),
        grid=(M // tm, nj),
        in_specs=[pl.BlockSpec((tm, RWKV_W), lambda i, j: (i, 0)),
                  pl.BlockSpec((tm, ATT_W), lambda i, j: (i, 0)),
                  pl.BlockSpec((RWKV_W, tn), lambda i, j: (0, j)),
                  pl.BlockSpec((ATT_W, tn), lambda i, j: (0, j)),
                  pl.BlockSpec((tm, tn), lambda i, j: (i, j0 + j)),
                  pl.BlockSpec((tm, tn), lambda i, j: (i, j0 + nj + j))],
        out_specs=pl.BlockSpec((tm, tn), lambda i, j: (i, j)),
        compiler_params=_params("parallel", "arbitrary"),
        name="branch_mix",
    )(y_r, y_a, w_r, w_a, f, f)


def _out_kernel(mix_ref, w_ref, x_ref, g_ref, b_ref, o_ref, *, tn):
    j = pl.program_id(1)
    col = pl.multiple_of(j * tn, tn)
    out = jnp.dot(mix_ref[...], w_ref[...], preferred_element_type=F32)
    o_ref[:, pl.ds(col, tn)] = ALPHA * x_ref[...] + out

    @pl.when(j == pl.num_programs(1) - 1)
    def _():
        h = o_ref[...]
        m = jnp.mean(h, axis=1, keepdims=True)
        d = h - m
        var = jnp.mean(d * d, axis=1, keepdims=True)
        o_ref[...] = d * lax.rsqrt(var + LN_EPS) * g_ref[...] + b_ref[...]


def _out_ln(mix, w_out, x, ln_g, ln_b):
    M, D = x.shape
    tm = _tile(M, (512, 256, 128))
    tn = _tile(D, (512, 256, 128))
    row = pl.BlockSpec((1, D), lambda i, j: (0, 0))
    return pl.pallas_call(
        functools.partial(_out_kernel, tn=tn),
        out_shape=jax.ShapeDtypeStruct((M, D), F32),
        grid=(M // tm, D // tn),
        in_specs=[pl.BlockSpec((tm, D), lambda i, j: (i, 0)),
                  pl.BlockSpec((D, tn), lambda i, j: (0, j)),
                  pl.BlockSpec((tm, tn), lambda i, j: (i, j)),
                  row, row],
        out_specs=pl.BlockSpec((tm, D), lambda i, j: (i, 0)),
        compiler_params=_params("parallel", "arbitrary"),
        name="out_ln",
    )(mix, w_out, x, ln_g, ln_b)


def _k_major(x, axis):
    shp = x.shape
    n = shp[axis] // RWKV_W
    y = x.reshape(shp[:axis] + (n, RWKV_HEADS, HEAD) + shp[axis + 1:])
    y = jnp.swapaxes(y, axis + 1, axis + 2)
    return y.reshape(shp)


def _head_major(x, axis):
    shp = x.shape
    n = shp[axis] // RWKV_W
    y = x.reshape(shp[:axis] + (n, HEAD, RWKV_HEADS) + shp[axis + 1:])
    y = jnp.swapaxes(y, axis + 1, axis + 2)
    return y.reshape(shp)


def _shift_k_major(x):
    return jnp.concatenate([_k_major(x[..., :3 * RWKV_W], x.ndim - 1), x[..., 3 * RWKV_W:]], axis=-1)


def _shift_head_major(x):
    return jnp.concatenate([_head_major(x[..., :3 * RWKV_W], x.ndim - 1), x[..., 3 * RWKV_W:]], axis=-1)


def _layer(x, S0, shift0, caches, w, p):
    B, T, D = x.shape
    M = B * T
    G = B // QUAD
    x2 = x.reshape(M, D)
    xb = x2.astype(BF16)
    zskv = _matmul(xb, w["zskv"], F32, name="proj_shift_kv")
    zskv3 = zskv.reshape(B, T, ZSKV_W)
    prep = _rwkv_prep(zskv3, _shift_k_major(shift0).reshape(B, 1, SHIFT_W), p)
    ops = [a.reshape(G, T, HEAD, LANES) for a in prep[:6]]
    s0 = S0.transpose(3, 2, 0, 1).reshape(HEAD, HEAD, B * RWKV_HEADS)
    f = _matmul_cols(xb, w["all"], W_Q, BF16, name="proj_q_gates")
    g_r = _matmul(xb, w["g_r"], BF16, name="proj_gate_r")
    yn, s_fin = _wkv_scan(ops, s0)
    y_r = _rwkv_gate(yn, prep[6], g_r.reshape(B, T, RWKV_W), p["gn_w"], p["gn_b"]).reshape(M, RWKV_W)
    S_new = s_fin.reshape(HEAD, HEAD, B, RWKV_HEADS).transpose(2, 3, 1, 0)
    shift_new = _shift_head_major(zskv3[:, T - 1, :SHIFT_W])

    if caches is None:
        y_a, k_rot = _attn_prompt(f, zskv, p["sinks"], B, T)
        k_buf = k_rot.reshape(B, T, N_KV_HEADS, HEAD)[:, T - WINDOW:]
        v_buf = zskv3[:, T - WINDOW:, SHIFT_W + KV_W:].reshape(B, WINDOW, N_KV_HEADS, HEAD)
    else:
        y_a, k_buf, v_buf = _attn_sample(f, zskv, p["sinks"], caches[0], caches[1], B, T)
        k_buf = k_buf.reshape(B, WINDOW, N_KV_HEADS, HEAD)
        v_buf = v_buf.reshape(B, WINDOW, N_KV_HEADS, HEAD)

    mix = _mix(y_r, y_a, w["branch_r"], w["branch_a"], f)
    y = _out_ln(mix, w["out"], x2, p["ln_g"], p["ln_b"]).reshape(B, T, D)
    return y, S_new, shift_new, k_buf, v_buf


def kernel(x_prompt, x_sample, state_wkv, state_shift, cache_win_k, cache_win_v, w_in, mu_shift, w0, w_decay_up, a0, w_aaa_up, k_k, k_a, r_k, gn_w, gn_b, sinks, w_branch_rwkv, w_branch_attn, w_out, ln_g, ln_b):
    assert w_in.shape[0] == DEPTH
    Bp = x_prompt.shape[0]
    Bs = x_sample.shape[0]
    assert Bp % QUAD == 0 and Bs % QUAD == 0
    o_gr = SHIFT_W
    o_k = W_Q + ATT_W
    o_ga = o_k + 2 * KV_W
    wi = w_in[0].astype(BF16)
    w = {"all": wi,
         "zskv": jnp.concatenate([_shift_k_major(wi[:, :o_gr]), wi[:, o_k:o_ga]], axis=1),
         "g_r": _k_major(wi[:, o_gr:W_Q], 1),
         "branch_r": _k_major(w_branch_rwkv[0], 0).astype(BF16),
         "branch_a": w_branch_attn[0].astype(BF16), "out": w_out[0].astype(BF16)}
    lane = jnp.arange(LANES)
    km = lambda a: _k_major(a.reshape(1, RWKV_W), 1)
    p = {"mu": _shift_k_major(mu_shift), "w0": km(w0), "wup": _k_major(w_decay_up[0], 1).astype(BF16),
         "a0": km(a0), "aup": _k_major(w_aaa_up[0], 1).astype(BF16), "k_k": km(k_k), "k_a": km(k_a),
         "r_k": km(r_k), "gn_w": km(gn_w), "gn_b": km(gn_b), "sinks": sinks[0], "ln_g": ln_g, "ln_b": ln_b,
         "jmat": (lane[:, None] % RWKV_HEADS == lane[None, :] % RWKV_HEADS).astype(BF16)}

    S0p = jnp.zeros((Bp, RWKV_HEADS, HEAD, HEAD), F32)
    sh0p = jnp.zeros((Bp, SHIFT_W), F32)
    yp, a1, a2, a3, a4 = _layer(x_prompt, S0p, sh0p, None, w, p)
    caches = (cache_win_k[0].reshape(Bs, WINDOW, KV_W), cache_win_v[0].reshape(Bs, WINDOW, KV_W))
    ys, b1, b2, b3, b4 = _layer(x_sample, state_wkv[0], state_shift[0], caches, w, p)
    return (yp, ys, a1[None], a2[None], a3[None], a4[None], b1[None], b2[None], b3[None], b4[None])
```

```python
import functools
import math

import jax
import jax.numpy as jnp
from jax import lax
from jax.experimental import pallas as pl
from jax.experimental.pallas import tpu as pltpu

RWKV_HEADS = 32
HEAD = 64
RWKV_W = RWKV_HEADS * HEAD
LORA = 128
SHIFT_W = 3 * RWKV_W + 2 * LORA
GN_EPS = 64e-5
N_Q_HEADS = 32
N_KV_HEADS = 4
ATT_W = N_Q_HEADS * HEAD
KV_W = N_KV_HEADS * HEAD
WINDOW = 128
ROT_DIM = HEAD // 4
ROPE_THETA = 500000.0
PAST_LEN = 8192
DEPTH = 1
ALPHA = (2.0 * DEPTH) ** 0.25
LN_EPS = 1e-5
LANES = 128
QUAD = LANES // RWKV_HEADS
DECAY_SCALE = math.exp(-0.5)
LOG2E = math.log2(math.e)
QK_SCALE = LOG2E / math.sqrt(HEAD)
ZSKV_W = SHIFT_W + 2 * KV_W
W_Q = SHIFT_W + RWKV_W
F_Q = 0
F_GA = ATT_W + 2 * KV_W
F_M = F_GA + ATT_W
F_GA_BLOCK = F_GA
W_TILE = 256
VMEM_LIMIT = 56 * 1024 * 1024

BF16 = jnp.bfloat16
F32 = jnp.float32


def _params(*sem):
    return pltpu.CompilerParams(dimension_semantics=sem, vmem_limit_bytes=VMEM_LIMIT)


def _tile(n, prefs):
    for p in prefs:
        if n % p == 0:
            return p
    return n


def _sigmoid(x):
    return 1.0 / (1.0 + jnp.exp(-x))


def _silu(x):
    return x * _sigmoid(x)


def _mm_kernel(x_ref, w_ref, o_ref):
    o_ref[...] = jnp.dot(x_ref[...], w_ref[...], preferred_element_type=F32).astype(o_ref.dtype)


def _matmul(x, w, out_dtype, name="proj"):
    M, K = x.shape
    N = w.shape[1]
    tm = _tile(M, (1024, 512, 256, 128))
    tn = _tile(N, (512, 768, 640, 256, 128))
    return pl.pallas_call(
        _mm_kernel,
        out_shape=jax.ShapeDtypeStruct((M, N), out_dtype),
        grid=(M // tm, N // tn),
        in_specs=[pl.BlockSpec((tm, K), lambda i, j: (i, 0)),
                  pl.BlockSpec((K, tn), lambda i, j: (0, j))],
        out_specs=pl.BlockSpec((tm, tn), lambda i, j: (i, j)),
        compiler_params=_params("parallel", "arbitrary"),
        name=name,
    )(x, w)


def _mm2_kernel(x_ref, wa_ref, wb_ref, o_ref):
    x = x_ref[...]
    o_ref[:, :W_TILE] = jnp.dot(x, wa_ref[...], preferred_element_type=F32).astype(o_ref.dtype)
    o_ref[:, W_TILE:] = jnp.dot(x, wb_ref[...], preferred_element_type=F32).astype(o_ref.dtype)


def _matmul_cols(x, w, col0, out_dtype, name="proj"):
    M, K = x.shape
    N = w.shape[1] - col0
    assert col0 % W_TILE == 0 and N % (2 * W_TILE) == 0
    tm = _tile(M, (1024, 512, 256, 128))
    j0 = col0 // W_TILE
    return pl.pallas_call(
        _mm2_kernel,
        out_shape=jax.ShapeDtypeStruct((M, N), out_dtype),
        grid=(M // tm, N // (2 * W_TILE)),
        in_specs=[pl.BlockSpec((tm, K), lambda i, j: (i, 0)),
                  pl.BlockSpec((K, W_TILE), lambda i, j: (0, j0 + 2 * j)),
                  pl.BlockSpec((K, W_TILE), lambda i, j: (0, j0 + 2 * j + 1))],
        out_specs=pl.BlockSpec((tm, 2 * W_TILE), lambda i, j: (i, j)),
        compiler_params=_params("parallel", "arbitrary"),
        name=name,
    )(x, w, w)


def _head_sum(x, j_ref):
    nt = x.shape[1] // LANES
    t = x[:, 0:LANES]
    for m in range(1, nt):
        t = t + x[:, m * LANES:(m + 1) * LANES]
    hi = t.astype(BF16)
    lo = (t - hi.astype(F32)).astype(BF16)
    s = (jnp.dot(hi, j_ref[...], preferred_element_type=F32)
         + jnp.dot(lo, j_ref[...], preferred_element_type=F32))
    return jnp.concatenate([s] * nt, axis=1)


def _lane_quarter_select(parts):
    lane = lax.broadcasted_iota(jnp.int32, parts[0].shape, 1)
    out = parts[QUAD - 1]
    for q in range(QUAD - 2, -1, -1):
        out = jnp.where(lane < (q + 1) * RWKV_HEADS, parts[q], out)
    return out


def _roll_lanes(x, shift):
    shift = shift % LANES
    return x if shift == 0 else pltpu.roll(x, shift, axis=1)


def _interleave_store(xs, o_ref):
    for m in range(RWKV_W // LANES):
        tiles = [x[:, m * LANES:(m + 1) * LANES] for x in xs]
        for j in range(QUAD):
            parts = [_roll_lanes(t, (b - j) * RWKV_HEADS) for b, t in enumerate(tiles)]
            k = m * QUAD + j
            o_ref[0, :, k * LANES:(k + 1) * LANES] = _lane_quarter_select(parts)


def _prep_kernel(zs_ref, sh_ref, mu_ref, w0_ref, wup_ref, a0_ref, aup_ref, kk_ref, ka_ref, rk_ref, j_ref,
                 r_o, w_o, kap_o, b_o, kt_o, v_o, bon_o, carry):
    @pl.when(pl.program_id(1) == 0)
    def _():
        carry[...] = sh_ref[...]

    tb = zs_ref.shape[1]
    row = lax.broadcasted_iota(jnp.int32, (tb, 1), 0)
    outs = [[] for _ in range(6)]
    for bq in range(QUAD):
        zs = zs_ref[bq]
        rolled = pltpu.roll(zs, 1, axis=0)
        prev = jnp.where(row == 0, carry[bq], rolled)
        carry[bq] = zs[tb - 1:tb, :]
        xs = zs + mu_ref[...] * (prev - zs)
        r = xs[:, 0:RWKV_W]
        k = xs[:, RWKV_W:2 * RWKV_W]
        v = xs[:, 2 * RWKV_W:3 * RWKV_W]
        wd = xs[:, 3 * RWKV_W:3 * RWKV_W + LORA]
        ad = xs[:, 3 * RWKV_W + LORA:SHIFT_W]
        wl = w0_ref[...] + jnp.dot(jnp.tanh(wd).astype(BF16), wup_ref[...], preferred_element_type=F32)
        dec = jnp.exp(-DECAY_SCALE * _sigmoid(wl))
        a = _sigmoid(a0_ref[...] + jnp.dot(ad.astype(BF16), aup_ref[...], preferred_element_type=F32))
        kkv = k * kk_ref[...]
        norm = jnp.maximum(jnp.sqrt(_head_sum(kkv * kkv, j_ref)), 1e-12)
        kap = kkv / norm
        kt = k * (1.0 + (a - 1.0) * ka_ref[...])
        bon_o[bq] = _head_sum(r * kt * rk_ref[...], j_ref) * v
        for lst, val in zip(outs, (r, dec, kap, kap * a, kt, v)):
            lst.append(val)
    for lst, o_ref in zip(outs, (r_o, w_o, kap_o, b_o, kt_o, v_o)):
        _interleave_store(lst, o_ref)


def _rwkv_prep(zs, shift0, p):
    B, T, _ = zs.shape
    G = B // QUAD
    tb = _tile(T, (32, 16, 8))
    row = lambda n: pl.BlockSpec((1, n), lambda g, i: (0, 0))
    full = lambda a: pl.BlockSpec(a.shape, lambda g, i: (0, 0))
    op = jax.ShapeDtypeStruct((G, T, HEAD * LANES), F32)
    opspec = pl.BlockSpec((1, tb, HEAD * LANES), lambda g, i: (g, i, 0))
    return pl.pallas_call(
        _prep_kernel,
        out_shape=(op,) * 6 + (jax.ShapeDtypeStruct((B, T, RWKV_W), F32),),
        grid=(G, T // tb),
        in_specs=[pl.BlockSpec((QUAD, tb, SHIFT_W), lambda g, i: (g, i, 0)),
                  pl.BlockSpec((QUAD, 1, SHIFT_W), lambda g, i: (g, 0, 0)),
                  row(SHIFT_W), row(RWKV_W), full(p["wup"]), row(RWKV_W), full(p["aup"]),
                  row(RWKV_W), row(RWKV_W), row(RWKV_W), full(p["jmat"])],
        out_specs=(opspec,) * 6 + (pl.BlockSpec((QUAD, tb, RWKV_W), lambda g, i: (g, i, 0)),),
        scratch_shapes=[pltpu.VMEM((QUAD, 1, SHIFT_W), F32)],
        compiler_params=_params("parallel", "arbitrary"),
        name="rwkv_prep",
    )(zs, shift0, p["mu"], p["w0"], p["wup"], p["a0"], p["aup"], p["k_k"], p["k_a"], p["r_k"], p["jmat"])


def _scan_kernel(r_ref, w_ref, kap_ref, b_ref, kt_ref, v_ref, s0_ref, y_ref, sout_ref, S):
    ti = pl.program_id(1)

    @pl.when(ti == 0)
    def _():
        S[...] = s0_ref[...]

    tb = r_ref.shape[1]

    u0 = S[0] * kap_ref[0, 0, 0:1, :]
    for k in range(1, HEAD):
        u0 = u0 + S[k] * kap_ref[0, 0, k:k + 1, :]

    def step(t, u):
        tn = jnp.minimum(t + 1, tb - 1)
        vt = v_ref[0, t]
        y = None
        un = None
        for k in range(HEAD):
            s = S[k] * w_ref[0, t, k:k + 1, :] - u * b_ref[0, t, k:k + 1, :] + vt * kt_ref[0, t, k:k + 1, :]
            S[k] = s
            ty = s * r_ref[0, t, k:k + 1, :]
            tu = s * kap_ref[0, tn, k:k + 1, :]
            y = ty if y is None else y + ty
            un = tu if un is None else un + tu
        m = jnp.mean(y, axis=0, keepdims=True)
        d = y - m
        var = jnp.mean(d * d, axis=0, keepdims=True)
        y_ref[0, t] = d * lax.rsqrt(var + GN_EPS)
        return un

    lax.fori_loop(0, tb, step, u0)

    @pl.when(ti == pl.num_programs(1) - 1)
    def _():
        sout_ref[...] = S[...]


def _wkv_scan(ops, s0):
    G, T = ops[0].shape[:2]
    tb = _tile(T, (32, 16, 8))
    ospec = pl.BlockSpec((1, tb, HEAD, LANES), lambda g, i: (g, i, 0, 0))
    sspec = pl.BlockSpec((HEAD, HEAD, LANES), lambda g, i: (0, 0, g))
    return pl.pallas_call(
        _scan_kernel,
        out_shape=(jax.ShapeDtypeStruct((G, T, HEAD, LANES), F32),
                   jax.ShapeDtypeStruct((HEAD, HEAD, G * LANES), F32)),
        grid=(G, T // tb),
        in_specs=[ospec] * 6 + [sspec],
        out_specs=(ospec, sspec),
        scratch_shapes=[pltpu.VMEM((HEAD, HEAD, LANES), F32)],
        compiler_params=_params("parallel", "arbitrary"),
        name="wkv_scan",
    )(*ops, s0)


def _scan_native_kernel(r_ref, w_ref, kap_ref, b_ref, kt_ref, v_ref, s0_ref, y_ref, sout_ref, S):
    ti = pl.program_id(1)

    @pl.when(ti == 0)
    def _():
        S[...] = s0_ref[...].T.reshape(HEAD, HEAD, LANES)

    tb = r_ref.shape[1]

    def step(t, c):
        for v in range(HEAD):
            s = S[v]
            u = jnp.sum(s * kap_ref[0, t], axis=0, keepdims=True)
            s = s * w_ref[0, t] - u * b_ref[0, t] + v_ref[0, t, v:v + 1, :] * kt_ref[0, t]
            S[v] = s
            y_ref[0, t, v:v + 1, :] = jnp.sum(s * r_ref[0, t], axis=0, keepdims=True)
        y = y_ref[0, t]
        m = jnp.mean(y, axis=0, keepdims=True)
        d = y - m
        var = jnp.mean(d * d, axis=0, keepdims=True)
        y_ref[0, t] = d * lax.rsqrt(var + GN_EPS)
        return c

    lax.fori_loop(0, tb, step, 0)

    @pl.when(ti == pl.num_programs(1) - 1)
    def _():
        sout_ref[...] = S[...].reshape(HEAD * HEAD, LANES).T


def _wkv_scan_native(ops, s0):
    G, T = ops[0].shape[:2]
    tb = _tile(T, (32, 16, 8))
    ospec = pl.BlockSpec((1, tb, HEAD, LANES), lambda g, i: (g, i, 0, 0))
    sspec = pl.BlockSpec((LANES, HEAD * HEAD), lambda g, i: (g, 0))
    return pl.pallas_call(
        _scan_native_kernel,
        out_shape=(jax.ShapeDtypeStruct((G, T, HEAD, LANES), F32),
                   jax.ShapeDtypeStruct((G * LANES, HEAD * HEAD), F32)),
        grid=(G, T // tb),
        in_specs=[ospec] * 6 + [sspec],
        out_specs=(ospec, sspec),
        scratch_shapes=[pltpu.VMEM((HEAD, HEAD, LANES), F32)],
        compiler_params=_params("parallel", "arbitrary"),
        name="wkv_scan_native",
    )(*ops, s0)


def _gate_kernel(yn_ref, bon_ref, g_ref, gw_ref, gb_ref, o_ref):
    for m in range(RWKV_W // LANES):
        tiles = [yn_ref[0, :, m * QUAD + j, :] for j in range(QUAD)]
        sl = slice(m * LANES, (m + 1) * LANES)
        for bq in range(QUAD):
            yn = _lane_quarter_select([_roll_lanes(t, (j - bq) * RWKV_HEADS) for j, t in enumerate(tiles)])
            y = yn * gw_ref[:, sl] + gb_ref[:, sl] + bon_ref[bq, :, sl]
            o_ref[bq, :, sl] = (y * _silu(g_ref[bq, :, sl].astype(F32))).astype(o_ref.dtype)


def _rwkv_gate(yn, bonus, g, gn_w, gn_b):
    B, T, _ = bonus.shape
    G = B // QUAD
    tb = _tile(T, (128, 64, 32, 16, 8))
    spec = pl.BlockSpec((QUAD, tb, RWKV_W), lambda g_, i: (g_, i, 0))
    row = pl.BlockSpec((1, RWKV_W), lambda g_, i: (0, 0))
    return pl.pallas_call(
        _gate_kernel,
        out_shape=jax.ShapeDtypeStruct((B, T, RWKV_W), BF16),
        grid=(G, T // tb),
        in_specs=[pl.BlockSpec((1, tb, HEAD, LANES), lambda g_, i: (g_, i, 0, 0)), spec, spec, row, row],
        out_specs=spec,
        compiler_params=_params("parallel", "parallel"),
        name="rwkv_gate",
    )(yn, bonus, g, gn_w, gn_b)


def _rope_tables(pos):
    half = ROT_DIM // 2
    inv = ROPE_THETA ** (-jnp.arange(half, dtype=F32) * 2.0 / ROT_DIM)
    ang = pos.astype(F32)[:, None] * inv[None, :]
    cos, sin = jnp.cos(ang), jnp.sin(ang)
    n = pos.shape[0]
    one = jnp.ones((n, HEAD - ROT_DIM), F32)
    zero = jnp.zeros((n, HEAD - ROT_DIM), F32)
    zh = jnp.zeros((n, half), F32)
    c = jnp.concatenate([cos, cos, one], axis=1)
    s1 = jnp.concatenate([zh, sin, zero], axis=1)
    s2 = jnp.concatenate([-sin, zh, zero], axis=1)
    reps = LANES // HEAD
    return jnp.tile(c, (1, reps)), jnp.tile(s1, (1, reps)), jnp.tile(s2, (1, reps))


def _rope(x, c, s1, s2):
    half = ROT_DIM // 2
    return x * c + pltpu.roll(x, half, axis=1) * s1 + pltpu.roll(x, LANES - half, axis=1) * s2


def _dup_heads(x):
    sw = pltpu.roll(x, HEAD, axis=1)
    low = lax.broadcasted_iota(jnp.int32, x.shape, 1) < HEAD
    return jnp.where(low, x, sw), jnp.where(low, sw, x)


def _softmax_with_sink(s, sink):
    m = jnp.maximum(jnp.max(s, axis=1, keepdims=True), sink)
    e = jnp.exp2(s - m)
    den = jnp.sum(e, axis=1, keepdims=True) + jnp.exp2(sink - m)
    return e * (1.0 / den)


def _pair_attention(q, k2, v2, mask, sink0, sink1):
    n = q.shape[0]
    low = lax.broadcasted_iota(jnp.int32, q.shape, 1) < HEAD
    q = q * QK_SCALE
    qs = jnp.concatenate([jnp.where(low, q, 0.0), jnp.where(low, 0.0, q)], axis=0).astype(BF16)
    s = lax.dot_general(qs, k2, (((1,), (1,)), ((), ())), preferred_element_type=F32)
    s = jnp.where(mask, s, -1e30)
    top = lax.broadcasted_iota(jnp.int32, (2 * n, 1), 0) < n
    sink = jnp.where(top, sink0, sink1) * LOG2E
    p = _softmax_with_sink(s, sink).astype(BF16)
    o = jnp.dot(p, v2, preferred_element_type=F32)
    return jnp.where(low, o[:n], o[n:])


def _attn_prompt_kernel(sink_ref, q_ref, kc_ref, kp_ref, vc_ref, vp_ref, g_ref,
                        cc_ref, s1c_ref, s2c_ref, cp_ref, s1p_ref, s2p_ref, o_ref, krot_ref):
    i = pl.program_id(1)
    W = WINDOW
    qi = lax.broadcasted_iota(jnp.int32, (2 * W, 2 * W), 0) % W
    kj = lax.broadcasted_iota(jnp.int32, (2 * W, 2 * W), 1)
    rel = W + qi - kj
    mask = (rel >= 0) & (rel < W) & ((i > 0) | (kj >= W))
    cc, s1c, s2c = cc_ref[...], s1c_ref[...], s2c_ref[...]
    cp, s1p, s2p = cp_ref[...], s1p_ref[...], s2p_ref[...]
    k2, v2 = [], []
    for t in range(KV_W // LANES):
        sl = slice(t * LANES, (t + 1) * LANES)
        kc = _rope(kc_ref[:, sl], cc, s1c, s2c)
        krot_ref[:, sl] = kc
        kcat = jnp.concatenate([_rope(kp_ref[:, sl], cp, s1p, s2p), kc], axis=0)
        vcat = jnp.concatenate([vp_ref[:, sl], vc_ref[:, sl]], axis=0)
        k2.extend(x.astype(BF16) for x in _dup_heads(kcat))
        v2.extend(x.astype(BF16) for x in _dup_heads(vcat))
    pairs_per_kv = N_Q_HEADS // N_KV_HEADS // 2
    for p in range(N_Q_HEADS // 2):
        sl = slice(p * LANES, (p + 1) * LANES)
        q = _rope(q_ref[:, sl].astype(F32), cc, s1c, s2c)
        g = p // pairs_per_kv
        o = _pair_attention(q, k2[g], v2[g], mask, sink_ref[2 * p], sink_ref[2 * p + 1])
        o_ref[:, sl] = (o * _silu(g_ref[:, sl].astype(F32))).astype(o_ref.dtype)


def _attn_prompt(f, zskv, sinks, B, T):
    W = WINDOW
    nb = T // W
    tabs = _rope_tables(jnp.arange(T))
    kcol = SHIFT_W // KV_W
    cur = lambda b, i: (b * nb + i, 0)
    tcur = pl.BlockSpec((W, LANES), lambda b, i: (i, 0))
    tprv = pl.BlockSpec((W, LANES), lambda b, i: (jnp.maximum(i - 1, 0), 0))
    return pl.pallas_call(
        _attn_prompt_kernel,
        out_shape=(jax.ShapeDtypeStruct((B * T, ATT_W), BF16), jax.ShapeDtypeStruct((B * T, KV_W), F32)),
        grid=(B, nb),
        in_specs=[pl.BlockSpec(memory_space=pltpu.SMEM),
                  pl.BlockSpec((W, ATT_W), lambda b, i: (b * nb + i, F_Q // ATT_W)),
                  pl.BlockSpec((W, KV_W), lambda b, i: (b * nb + i, kcol)),
                  pl.BlockSpec((W, KV_W), lambda b, i: (b * nb + jnp.maximum(i - 1, 0), kcol)),
                  pl.BlockSpec((W, KV_W), lambda b, i: (b * nb + i, kcol + 1)),
                  pl.BlockSpec((W, KV_W), lambda b, i: (b * nb + jnp.maximum(i - 1, 0), kcol + 1)),
                  pl.BlockSpec((W, F_GA_BLOCK), lambda b, i: (b * nb + i, F_GA // F_GA_BLOCK)),
                  tcur, tcur, tcur, tprv, tprv, tprv],
        out_specs=(pl.BlockSpec((W, ATT_W), cur), pl.BlockSpec((W, KV_W), cur)),
        compiler_params=_params("parallel", "parallel"),
        name="attn_prompt",
    )(sinks, f, zskv, zskv, zskv, zskv, f, *tabs, *tabs)


def _attn_sample_kernel(q_ref, kn_ref, vn_ref, ck_ref, cv_ref, g_ref, sink_ref, c_ref, s1_ref, s2_ref,
                        o_ref, ko_ref, vo_ref):
    W = WINDOW
    nbat = ck_ref.shape[0]
    Ts = q_ref.shape[0] // nbat
    GQ = N_Q_HEADS // N_KV_HEADS
    GW = GQ * HEAD
    nrow = N_Q_HEADS * Ts
    tq = lax.broadcasted_iota(jnp.int32, (nrow, W + Ts), 0) % Ts
    kj = lax.broadcasted_iota(jnp.int32, (nrow, W + Ts), 1)
    rel = W + tq - kj
    mask = (rel >= 0) & (rel < W)
    sink = sink_ref[...] * LOG2E
    lane_head = lax.broadcasted_iota(jnp.int32, (Ts, GW), 1) // HEAD
    c, s1, s2 = c_ref[...], s1_ref[...], s2_ref[...]
    tiles_per_group = GW // LANES
    for bi in range(nbat):
        rows = slice(bi * Ts, (bi + 1) * Ts)
        k2, v2 = [], []
        for t in range(KV_W // LANES):
            sl = slice(t * LANES, (t + 1) * LANES)
            kcat = jnp.concatenate([ck_ref[bi, :, sl], _rope(kn_ref[rows, sl], c, s1, s2)], axis=0)
            vcat = jnp.concatenate([cv_ref[bi, :, sl], vn_ref[rows, sl]], axis=0)
            ko_ref[bi, :, sl] = kcat[Ts:, :]
            vo_ref[bi, :, sl] = vcat[Ts:, :]
            k2.extend(x.astype(BF16) for x in _dup_heads(kcat))
            v2.extend(x.astype(BF16) for x in _dup_heads(vcat))
        s_parts = []
        for g in range(N_KV_HEADS):
            qg = jnp.concatenate(
                [_rope(q_ref[rows, (g * tiles_per_group + i) * LANES:(g * tiles_per_group + i + 1) * LANES]
                       .astype(F32), c, s1, s2)
                 for i in range(tiles_per_group)], axis=1)
            qg = qg * QK_SCALE
            qs = jnp.concatenate([jnp.where(lane_head == h, qg, 0.0) for h in range(GQ)], axis=0).astype(BF16)
            k8 = jnp.concatenate([k2[g]] * tiles_per_group, axis=1)
            s_parts.append(lax.dot_general(qs, k8, (((1,), (1,)), ((), ())), preferred_element_type=F32))
        s = jnp.where(mask, jnp.concatenate(s_parts, axis=0), -1e30)
        p = _softmax_with_sink(s, sink).astype(BF16)
        for g in range(N_KV_HEADS):
            v8 = jnp.concatenate([v2[g]] * tiles_per_group, axis=1)
            o = jnp.dot(p[g * GQ * Ts:(g + 1) * GQ * Ts], v8, preferred_element_type=F32)
            og = o[0:Ts]
            for h in range(1, GQ):
                og = jnp.where(lane_head == h, o[h * Ts:(h + 1) * Ts], og)
            gl = slice(g * GW, (g + 1) * GW)
            o_ref[rows, gl] = (og * _silu(g_ref[rows, gl].astype(F32))).astype(o_ref.dtype)


def _attn_sample(f, zskv, sinks, cache_k, cache_v, B, Ts):
    W = WINDOW
    nbat = _tile(B, (4, 2, 1))
    tabs = _rope_tables(PAST_LEN + jnp.arange(Ts))
    kcol = SHIFT_W // KV_W
    rows = nbat * Ts
    sink_col = jnp.repeat(sinks, Ts).reshape(N_Q_HEADS * Ts, 1)
    cspec = pl.BlockSpec((nbat, W, KV_W), lambda i: (i, 0, 0))
    tspec = pl.BlockSpec((Ts, LANES), lambda i: (0, 0))
    cache = jax.ShapeDtypeStruct((B, W, KV_W), F32)
    return pl.pallas_call(
        _attn_sample_kernel,
        out_shape=(jax.ShapeDtypeStruct((B * Ts, ATT_W), BF16), cache, cache),
        grid=(B // nbat,),
        in_specs=[pl.BlockSpec((rows, ATT_W), lambda i: (i, F_Q // ATT_W)),
                  pl.BlockSpec((rows, KV_W), lambda i: (i, kcol)),
                  pl.BlockSpec((rows, KV_W), lambda i: (i, kcol + 1)),
                  cspec, cspec,
                  pl.BlockSpec((rows, F_GA_BLOCK), lambda i: (i, F_GA // F_GA_BLOCK)),
                  pl.BlockSpec((N_Q_HEADS * Ts, 1), lambda i: (0, 0)),
                  tspec, tspec, tspec],
        out_specs=(pl.BlockSpec((rows, ATT_W), lambda i: (i, 0)), cspec, cspec),
        compiler_params=_params("parallel"),
        name="attn_sample",
    )(f, zskv, zskv, cache_k, cache_v, f, sink_col, *tabs)


def _mix_kernel(yr_ref, ya_ref, wr_ref, wa_ref, gr_ref, ga_ref, o_ref):
    br = jnp.dot(yr_ref[...], wr_ref[...], preferred_element_type=F32)
    ba = jnp.dot(ya_ref[...], wa_ref[...], preferred_element_type=F32)
    o_ref[...] = (_sigmoid(gr_ref[...].astype(F32)) * br
                  + _sigmoid(ga_ref[...].astype(F32)) * ba).astype(o_ref.dtype)


def _mix(y_r, y_a, w_r, w_a, f):
    M = y_r.shape[0]
    D = w_r.shape[1]
    tm = _tile(M, (1024, 512, 256, 128))
    tn = _tile(D, (512, 256, 128))
    nj = D // tn
    j0 = F_M // tn
    return pl.pallas_call(
        _mix_kernel,
        out_shape=jax.ShapeDtypeStruct((M, D), BF16),
        grid=(M // tm, nj),
        in_specs=[pl.BlockSpec((tm, RWKV_W), lambda i, j: (i, 0)),
                  pl.BlockSpec((tm, ATT_W), lambda i, j: (i, 0)),
                  pl.BlockSpec((RWKV_W, tn), lambda i, j: (0, j)),
                  pl.BlockSpec((ATT_W, tn), lambda i, j: (0, j)),
                  pl.BlockSpec((tm, tn), lambda i, j: (i, j0 + j)),
                  pl.BlockSpec((tm, tn), lambda i, j: (i, j0 + nj + j))],
        out_specs=pl.BlockSpec((tm, tn), lambda i, j: (i, j)),
        compiler_params=_params("parallel", "arbitrary"),
        name="branch_mix",
    )(y_r, y_a, w_r, w_a, f, f)


def _out_kernel(mix_ref, w_ref, x_ref, g_ref, b_ref, o_ref, *, tn):
    j = pl.program_id(1)
    col = pl.multiple_of(j * tn, tn)
    out = jnp.dot(mix_ref[...], w_ref[...], preferred_element_type=F32)
    o_ref[:, pl.ds(col, tn)] = ALPHA * x_ref[...] + out

    @pl.when(j == pl.num_programs(1) - 1)
    def _():
        h = o_ref[...]
        m = jnp.mean(h, axis=1, keepdims=True)
        d = h - m
        var = jnp.mean(d * d, axis=1, keepdims=True)
        o_ref[...] = d * lax.rsqrt(var + LN_EPS) * g_ref[...] + b_ref[...]


def _out_ln(mix, w_out, x, ln_g, ln_b):
    M, D = x.shape
    tm = _tile(M, (512, 256, 128))
    tn = _tile(D, (512, 256, 128))
    row = pl.BlockSpec((1, D), lambda i, j: (0, 0))
    return pl.pallas_call(
        functools.partial(_out_kernel, tn=tn),
        out_shape=jax.ShapeDtypeStruct((M, D), F32),
        grid=(M // tm, D // tn),
        in_specs=[pl.BlockSpec((tm, D), lambda i, j: (i, 0)),
                  pl.BlockSpec((D, tn), lambda i, j: (0, j)),
                  pl.BlockSpec((tm, tn), lambda i, j: (i, j)),
                  row, row],
        out_specs=pl.BlockSpec((tm, D), lambda i, j: (i, 0)),
        compiler_params=_params("parallel", "arbitrary"),
        name="out_ln",
    )(mix, w_out, x, ln_g, ln_b)


def _k_major(x, axis):
    shp = x.shape
    n = shp[axis] // RWKV_W
    y = x.reshape(shp[:axis] + (n, RWKV_HEADS, HEAD) + shp[axis + 1:])
    y = jnp.swapaxes(y, axis + 1, axis + 2)
    return y.reshape(shp)


def _head_major(x, axis):
    shp = x.shape
    n = shp[axis] // RWKV_W
    y = x.reshape(shp[:axis] + (n, HEAD, RWKV_HEADS) + shp[axis + 1:])
    y = jnp.swapaxes(y, axis + 1, axis + 2)
    return y.reshape(shp)


def _shift_k_major(x):
    return jnp.concatenate([_k_major(x[..., :3 * RWKV_W], x.ndim - 1), x[..., 3 * RWKV_W:]], axis=-1)


def _shift_head_major(x):
    return jnp.concatenate([_head_major(x[..., :3 * RWKV_W], x.ndim - 1), x[..., 3 * RWKV_W:]], axis=-1)


def _layer(x, S0, shift0, caches, w, p):
    B, T, D = x.shape
    M = B * T
    G = B // QUAD
    x2 = x.reshape(M, D)
    xb = x2.astype(BF16)
    zskv = _matmul(xb, w["zskv"], F32, name="proj_shift_kv")
    zskv3 = zskv.reshape(B, T, ZSKV_W)
    prep = _rwkv_prep(zskv3, _shift_k_major(shift0).reshape(B, 1, SHIFT_W), p)
    ops = [a.reshape(G, T, HEAD, LANES) for a in prep[:6]]
    f = _matmul_cols(xb, w["all"], W_Q, BF16, name="proj_q_gates")
    g_r = _matmul(xb, w["g_r"], BF16, name="proj_gate_r")
    if S0 is None:
        yn, s_fin = _wkv_scan(ops, jnp.zeros((HEAD, HEAD, B * RWKV_HEADS), F32))
        S_new = s_fin.reshape(HEAD, HEAD, B, RWKV_HEADS).transpose(2, 3, 1, 0)
    else:
        yn, s_fin = _wkv_scan_native(ops, S0.reshape(B * RWKV_HEADS, HEAD * HEAD))
        S_new = s_fin.reshape(B, RWKV_HEADS, HEAD, HEAD)
    y_r = _rwkv_gate(yn, prep[6], g_r.reshape(B, T, RWKV_W), p["gn_w"], p["gn_b"]).reshape(M, RWKV_W)
    shift_new = _shift_head_major(zskv3[:, T - 1, :SHIFT_W])

    if caches is None:
        y_a, k_rot = _attn_prompt(f, zskv, p["sinks"], B, T)
        k_buf = k_rot.reshape(B, T, N_KV_HEADS, HEAD)[:, T - WINDOW:]
        v_buf = zskv3[:, T - WINDOW:, SHIFT_W + KV_W:].reshape(B, WINDOW, N_KV_HEADS, HEAD)
    else:
        y_a, k_buf, v_buf = _attn_sample(f, zskv, p["sinks"], caches[0], caches[1], B, T)
        k_buf = k_buf.reshape(B, WINDOW, N_KV_HEADS, HEAD)
        v_buf = v_buf.reshape(B, WINDOW, N_KV_HEADS, HEAD)

    mix = _mix(y_r, y_a, w["branch_r"], w["branch_a"], f)
    y = _out_ln(mix, w["out"], x2, p["ln_g"], p["ln_b"]).reshape(B, T, D)
    return y, S_new, shift_new, k_buf, v_buf


def kernel(x_prompt, x_sample, state_wkv, state_shift, cache_win_k, cache_win_v, w_in, mu_shift, w0, w_decay_up, a0, w_aaa_up, k_k, k_a, r_k, gn_w, gn_b, sinks, w_branch_rwkv, w_branch_attn, w_out, ln_g, ln_b):
    assert w_in.shape[0] == DEPTH
    Bp = x_prompt.shape[0]
    Bs = x_sample.shape[0]
    assert Bp % QUAD == 0 and Bs % QUAD == 0
    o_gr = SHIFT_W
    o_k = W_Q + ATT_W
    o_ga = o_k + 2 * KV_W
    wi = w_in[0].astype(BF16)
    w = {"all": wi,
         "zskv": jnp.concatenate([_shift_k_major(wi[:, :o_gr]), wi[:, o_k:o_ga]], axis=1),
         "g_r": _k_major(wi[:, o_gr:W_Q], 1),
         "branch_r": _k_major(w_branch_rwkv[0], 0).astype(BF16),
         "branch_a": w_branch_attn[0].astype(BF16), "out": w_out[0].astype(BF16)}
    lane = jnp.arange(LANES)
    km = lambda a: _k_major(a.reshape(1, RWKV_W), 1)
    p = {"mu": _shift_k_major(mu_shift), "w0": km(w0), "wup": _k_major(w_decay_up[0], 1).astype(BF16),
         "a0": km(a0), "aup": _k_major(w_aaa_up[0], 1).astype(BF16), "k_k": km(k_k), "k_a": km(k_a),
         "r_k": km(r_k), "gn_w": km(gn_w), "gn_b": km(gn_b), "sinks": sinks[0], "ln_g": ln_g, "ln_b": ln_b,
         "jmat": (lane[:, None] % RWKV_HEADS == lane[None, :] % RWKV_HEADS).astype(BF16)}

    sh0p = jnp.zeros((Bp, SHIFT_W), F32)
    yp, a1, a2, a3, a4 = _layer(x_prompt, None, sh0p, None, w, p)
    caches = (cache_win_k[0].reshape(Bs, WINDOW, KV_W), cache_win_v[0].reshape(Bs, WINDOW, KV_W))
    ys, b1, b2, b3, b4 = _layer(x_sample, state_wkv[0], state_shift[0], caches, w, p)
    return (yp, ys, a1[None], a2[None], a3[None], a4[None], b1[None], b2[None], b3[None], b4[None])
```

```python
import functools
import math

import jax
import jax.numpy as jnp
from jax import lax
from jax.experimental import pallas as pl
from jax.experimental.pallas import tpu as pltpu

RWKV_HEADS = 32
HEAD = 64
RWKV_W = RWKV_HEADS * HEAD
LORA = 128
SHIFT_W = 3 * RWKV_W + 2 * LORA
GN_EPS = 64e-5
N_Q_HEADS = 32
N_KV_HEADS = 4
ATT_W = N_Q_HEADS * HEAD
KV_W = N_KV_HEADS * HEAD
WINDOW = 128
ROT_DIM = HEAD // 4
ROPE_THETA = 500000.0
PAST_LEN = 8192
DEPTH = 1
ALPHA = (2.0 * DEPTH) ** 0.25
LN_EPS = 1e-5
LANES = 128
QUAD = LANES // RWKV_HEADS
DECAY_SCALE = math.exp(-0.5)
LOG2E = math.log2(math.e)
QK_SCALE = LOG2E / math.sqrt(HEAD)
ZSKV_W = SHIFT_W + 2 * KV_W
W_Q = SHIFT_W + RWKV_W
F_Q = 0
F_GA = ATT_W + 2 * KV_W
F_M = F_GA + ATT_W
F_GA_BLOCK = F_GA
W_TILE = 256
VMEM_LIMIT = 56 * 1024 * 1024

BF16 = jnp.bfloat16
F32 = jnp.float32


def _params(*sem):
    return pltpu.CompilerParams(dimension_semantics=sem, vmem_limit_bytes=VMEM_LIMIT)


def _tile(n, prefs):
    for p in prefs:
        if n % p == 0:
            return p
    return n


def _sigmoid(x):
    return 1.0 / (1.0 + jnp.exp(-x))


def _silu(x):
    return x * _sigmoid(x)


def _mm_kernel(x_ref, w_ref, o_ref):
    o_ref[...] = jnp.dot(x_ref[...], w_ref[...], preferred_element_type=F32).astype(o_ref.dtype)


def _matmul(x, w, out_dtype, name="proj"):
    M, K = x.shape
    N = w.shape[1]
    tm = _tile(M, (1024, 512, 256, 128))
    tn = _tile(N, (512, 768, 640, 256, 128))
    return pl.pallas_call(
        _mm_kernel,
        out_shape=jax.ShapeDtypeStruct((M, N), out_dtype),
        grid=(M // tm, N // tn),
        in_specs=[pl.BlockSpec((tm, K), lambda i, j: (i, 0)),
                  pl.BlockSpec((K, tn), lambda i, j: (0, j))],
        out_specs=pl.BlockSpec((tm, tn), lambda i, j: (i, j)),
        compiler_params=_params("parallel", "arbitrary"),
        name=name,
    )(x, w)


def _mm2_kernel(x_ref, wa_ref, wb_ref, o_ref):
    x = x_ref[...]
    o_ref[:, :W_TILE] = jnp.dot(x, wa_ref[...], preferred_element_type=F32).astype(o_ref.dtype)
    o_ref[:, W_TILE:] = jnp.dot(x, wb_ref[...], preferred_element_type=F32).astype(o_ref.dtype)


def _matmul_cols(x, w, col0, out_dtype, name="proj"):
    M, K = x.shape
    N = w.shape[1] - col0
    assert col0 % W_TILE == 0 and N % (2 * W_TILE) == 0
    tm = _tile(M, (1024, 512, 256, 128))
    j0 = col0 // W_TILE
    return pl.pallas_call(
        _mm2_kernel,
        out_shape=jax.ShapeDtypeStruct((M, N), out_dtype),
        grid=(M // tm, N // (2 * W_TILE)),
        in_specs=[pl.BlockSpec((tm, K), lambda i, j: (i, 0)),
                  pl.BlockSpec((K, W_TILE), lambda i, j: (0, j0 + 2 * j)),
                  pl.BlockSpec((K, W_TILE), lambda i, j: (0, j0 + 2 * j + 1))],
        out_specs=pl.BlockSpec((tm, 2 * W_TILE), lambda i, j: (i, j)),
        compiler_params=_params("parallel", "arbitrary"),
        name=name,
    )(x, w, w)


def _head_sum(x, j_ref):
    nt = x.shape[1] // LANES
    t = x[:, 0:LANES]
    for m in range(1, nt):
        t = t + x[:, m * LANES:(m + 1) * LANES]
    hi = t.astype(BF16)
    lo = (t - hi.astype(F32)).astype(BF16)
    s = (jnp.dot(hi, j_ref[...], preferred_element_type=F32)
         + jnp.dot(lo, j_ref[...], preferred_element_type=F32))
    return jnp.concatenate([s] * nt, axis=1)


def _lane_quarter_select(parts):
    lane = lax.broadcasted_iota(jnp.int32, parts[0].shape, 1)
    out = parts[QUAD - 1]
    for q in range(QUAD - 2, -1, -1):
        out = jnp.where(lane < (q + 1) * RWKV_HEADS, parts[q], out)
    return out


def _roll_lanes(x, shift):
    shift = shift % LANES
    return x if shift == 0 else pltpu.roll(x, shift, axis=1)


def _quad_transpose(tiles):
    rolled = []
    for s in range(QUAD):
        gathered = _lane_quarter_select([tiles[(g + s) % QUAD] for g in range(QUAD)])
        rolled.append(_roll_lanes(gathered, s * RWKV_HEADS))
    return [_lane_quarter_select([rolled[(a - c) % QUAD] for a in range(QUAD)]) for c in range(QUAD)]


def _interleave_store(xs, o_ref):
    for m in range(RWKV_W // LANES):
        outs = _quad_transpose([x[:, m * LANES:(m + 1) * LANES] for x in xs])
        for j in range(QUAD):
            k = m * QUAD + j
            o_ref[0, :, k * LANES:(k + 1) * LANES] = outs[j]


def _prep_kernel(zs_ref, sh_ref, mu_ref, w0_ref, wup_ref, a0_ref, aup_ref, kk_ref, ka_ref, rk_ref, j_ref,
                 r_o, w_o, kap_o, b_o, kt_o, v_o, bon_o, carry):
    @pl.when(pl.program_id(1) == 0)
    def _():
        carry[...] = sh_ref[...]

    tb = zs_ref.shape[1]
    row = lax.broadcasted_iota(jnp.int32, (tb, 1), 0)
    outs = [[] for _ in range(6)]
    for bq in range(QUAD):
        zs = zs_ref[bq]
        rolled = pltpu.roll(zs, 1, axis=0)
        prev = jnp.where(row == 0, carry[bq], rolled)
        carry[bq] = zs[tb - 1:tb, :]
        xs = zs + mu_ref[...] * (prev - zs)
        r = xs[:, 0:RWKV_W]
        k = xs[:, RWKV_W:2 * RWKV_W]
        v = xs[:, 2 * RWKV_W:3 * RWKV_W]
        wd = xs[:, 3 * RWKV_W:3 * RWKV_W + LORA]
        ad = xs[:, 3 * RWKV_W + LORA:SHIFT_W]
        wl = w0_ref[...] + jnp.dot(jnp.tanh(wd).astype(BF16), wup_ref[...], preferred_element_type=F32)
        dec = jnp.exp(-DECAY_SCALE * _sigmoid(wl))
        a = _sigmoid(a0_ref[...] + jnp.dot(ad.astype(BF16), aup_ref[...], preferred_element_type=F32))
        kkv = k * kk_ref[...]
        norm = jnp.maximum(jnp.sqrt(_head_sum(kkv * kkv, j_ref)), 1e-12)
        kap = kkv / norm
        kt = k * (1.0 + (a - 1.0) * ka_ref[...])
        bon_o[bq] = _head_sum(r * kt * rk_ref[...], j_ref) * v
        for lst, val in zip(outs, (r, dec, kap, kap * a, kt, v)):
            lst.append(val)
    for lst, o_ref in zip(outs, (r_o, w_o, kap_o, b_o, kt_o, v_o)):
        _interleave_store(lst, o_ref)


def _rwkv_prep(zs, shift0, p):
    B, T, _ = zs.shape
    G = B // QUAD
    tb = _tile(T, (32, 16, 8))
    row = lambda n: pl.BlockSpec((1, n), lambda g, i: (0, 0))
    full = lambda a: pl.BlockSpec(a.shape, lambda g, i: (0, 0))
    op = jax.ShapeDtypeStruct((G, T, HEAD * LANES), F32)
    opspec = pl.BlockSpec((1, tb, HEAD * LANES), lambda g, i: (g, i, 0))
    return pl.pallas_call(
        _prep_kernel,
        out_shape=(op,) * 6 + (jax.ShapeDtypeStruct((B, T, RWKV_W), F32),),
        grid=(G, T // tb),
        in_specs=[pl.BlockSpec((QUAD, tb, SHIFT_W), lambda g, i: (g, i, 0)),
                  pl.BlockSpec((QUAD, 1, SHIFT_W), lambda g, i: (g, 0, 0)),
                  row(SHIFT_W), row(RWKV_W), full(p["wup"]), row(RWKV_W), full(p["aup"]),
                  row(RWKV_W), row(RWKV_W), row(RWKV_W), full(p["jmat"])],
        out_specs=(opspec,) * 6 + (pl.BlockSpec((QUAD, tb, RWKV_W), lambda g, i: (g, i, 0)),),
        scratch_shapes=[pltpu.VMEM((QUAD, 1, SHIFT_W), F32)],
        compiler_params=_params("parallel", "arbitrary"),
        name="rwkv_prep",
    )(zs, shift0, p["mu"], p["w0"], p["wup"], p["a0"], p["aup"], p["k_k"], p["k_a"], p["r_k"], p["jmat"])


def _scan_kernel(r_ref, w_ref, kap_ref, b_ref, kt_ref, v_ref, s0_ref, y_ref, sout_ref, S):
    ti = pl.program_id(1)

    @pl.when(ti == 0)
    def _():
        S[...] = s0_ref[...]

    tb = r_ref.shape[1]

    u0 = S[0] * kap_ref[0, 0, 0:1, :]
    for k in range(1, HEAD):
        u0 = u0 + S[k] * kap_ref[0, 0, k:k + 1, :]

    def step(t, u):
        tn = jnp.minimum(t + 1, tb - 1)
        vt = v_ref[0, t]
        y = None
        un = None
        for k in range(HEAD):
            s = S[k] * w_ref[0, t, k:k + 1, :] - u * b_ref[0, t, k:k + 1, :] + vt * kt_ref[0, t, k:k + 1, :]
            S[k] = s
            ty = s * r_ref[0, t, k:k + 1, :]
            tu = s * kap_ref[0, tn, k:k + 1, :]
            y = ty if y is None else y + ty
            un = tu if un is None else un + tu
        m = jnp.mean(y, axis=0, keepdims=True)
        d = y - m
        var = jnp.mean(d * d, axis=0, keepdims=True)
        y_ref[0, t] = d * lax.rsqrt(var + GN_EPS)
        return un

    lax.fori_loop(0, tb, step, u0)

    @pl.when(ti == pl.num_programs(1) - 1)
    def _():
        sout_ref[...] = S[...]


def _wkv_scan(ops, s0):
    G, T = ops[0].shape[:2]
    tb = _tile(T, (32, 16, 8))
    ospec = pl.BlockSpec((1, tb, HEAD, LANES), lambda g, i: (g, i, 0, 0))
    sspec = pl.BlockSpec((HEAD, HEAD, LANES), lambda g, i: (0, 0, g))
    return pl.pallas_call(
        _scan_kernel,
        out_shape=(jax.ShapeDtypeStruct((G, T, HEAD, LANES), F32),
                   jax.ShapeDtypeStruct((HEAD, HEAD, G * LANES), F32)),
        grid=(G, T // tb),
        in_specs=[ospec] * 6 + [sspec],
        out_specs=(ospec, sspec),
        scratch_shapes=[pltpu.VMEM((HEAD, HEAD, LANES), F32)],
        compiler_params=_params("parallel", "arbitrary"),
        name="wkv_scan",
    )(*ops, s0)


def _gate_kernel(yn_ref, bon_ref, g_ref, gw_ref, gb_ref, o_ref):
    for m in range(RWKV_W // LANES):
        yns = _quad_transpose([yn_ref[0, :, m * QUAD + j, :] for j in range(QUAD)])
        sl = slice(m * LANES, (m + 1) * LANES)
        for bq in range(QUAD):
            y = yns[bq] * gw_ref[:, sl] + gb_ref[:, sl] + bon_ref[bq, :, sl]
            o_ref[bq, :, sl] = (y * _silu(g_ref[bq, :, sl].astype(F32))).astype(o_ref.dtype)


def _rwkv_gate(yn, bonus, g, gn_w, gn_b):
    B, T, _ = bonus.shape
    G = B // QUAD
    tb = _tile(T, (128, 64, 32, 16, 8))
    spec = pl.BlockSpec((QUAD, tb, RWKV_W), lambda g_, i: (g_, i, 0))
    row = pl.BlockSpec((1, RWKV_W), lambda g_, i: (0, 0))
    return pl.pallas_call(
        _gate_kernel,
        out_shape=jax.ShapeDtypeStruct((B, T, RWKV_W), BF16),
        grid=(G, T // tb),
        in_specs=[pl.BlockSpec((1, tb, HEAD, LANES), lambda g_, i: (g_, i, 0, 0)), spec, spec, row, row],
        out_specs=spec,
        compiler_params=_params("parallel", "parallel"),
        name="rwkv_gate",
    )(yn, bonus, g, gn_w, gn_b)


def _rope_tables(pos):
    half = ROT_DIM // 2
    inv = ROPE_THETA ** (-jnp.arange(half, dtype=F32) * 2.0 / ROT_DIM)
    ang = pos.astype(F32)[:, None] * inv[None, :]
    cos, sin = jnp.cos(ang), jnp.sin(ang)
    n = pos.shape[0]
    one = jnp.ones((n, HEAD - ROT_DIM), F32)
    zero = jnp.zeros((n, HEAD - ROT_DIM), F32)
    zh = jnp.zeros((n, half), F32)
    c = jnp.concatenate([cos, cos, one], axis=1)
    s1 = jnp.concatenate([zh, sin, zero], axis=1)
    s2 = jnp.concatenate([-sin, zh, zero], axis=1)
    reps = LANES // HEAD
    return jnp.tile(c, (1, reps)), jnp.tile(s1, (1, reps)), jnp.tile(s2, (1, reps))


def _rope(x, c, s1, s2):
    half = ROT_DIM // 2
    return x * c + pltpu.roll(x, half, axis=1) * s1 + pltpu.roll(x, LANES - half, axis=1) * s2


def _dup_heads(x):
    sw = pltpu.roll(x, HEAD, axis=1)
    low = lax.broadcasted_iota(jnp.int32, x.shape, 1) < HEAD
    return jnp.where(low, x, sw), jnp.where(low, sw, x)


def _softmax_with_sink(s, sink):
    m = jnp.maximum(jnp.max(s, axis=1, keepdims=True), sink)
    e = jnp.exp2(s - m)
    den = jnp.sum(e, axis=1, keepdims=True) + jnp.exp2(sink - m)
    return e * (1.0 / den)


def _pair_attention(q, k2, v2, mask, sink0, sink1):
    n = q.shape[0]
    low = lax.broadcasted_iota(jnp.int32, q.shape, 1) < HEAD
    q = q * QK_SCALE
    qs = jnp.concatenate([jnp.where(low, q, 0.0), jnp.where(low, 0.0, q)], axis=0).astype(BF16)
    s = lax.dot_general(qs, k2, (((1,), (1,)), ((), ())), preferred_element_type=F32)
    s = jnp.where(mask, s, -1e30)
    top = lax.broadcasted_iota(jnp.int32, (2 * n, 1), 0) < n
    sink = jnp.where(top, sink0, sink1) * LOG2E
    p = _softmax_with_sink(s, sink).astype(BF16)
    o = jnp.dot(p, v2, preferred_element_type=F32)
    return jnp.where(low, o[:n], o[n:])


def _attn_prompt_kernel(sink_ref, q_ref, kc_ref, kp_ref, vc_ref, vp_ref, g_ref,
                        cc_ref, s1c_ref, s2c_ref, cp_ref, s1p_ref, s2p_ref, o_ref, krot_ref):
    i = pl.program_id(1)
    W = WINDOW
    qi = lax.broadcasted_iota(jnp.int32, (2 * W, 2 * W), 0) % W
    kj = lax.broadcasted_iota(jnp.int32, (2 * W, 2 * W), 1)
    rel = W + qi - kj
    mask = (rel >= 0) & (rel < W) & ((i > 0) | (kj >= W))
    cc, s1c, s2c = cc_ref[...], s1c_ref[...], s2c_ref[...]
    cp, s1p, s2p = cp_ref[...], s1p_ref[...], s2p_ref[...]
    k2, v2 = [], []
    for t in range(KV_W // LANES):
        sl = slice(t * LANES, (t + 1) * LANES)
        kc = _rope(kc_ref[:, sl], cc, s1c, s2c)
        krot_ref[:, sl] = kc
        kcat = jnp.concatenate([_rope(kp_ref[:, sl], cp, s1p, s2p), kc], axis=0)
        vcat = jnp.concatenate([vp_ref[:, sl], vc_ref[:, sl]], axis=0)
        k2.extend(x.astype(BF16) for x in _dup_heads(kcat))
        v2.extend(x.astype(BF16) for x in _dup_heads(vcat))
    pairs_per_kv = N_Q_HEADS // N_KV_HEADS // 2
    for p in range(N_Q_HEADS // 2):
        sl = slice(p * LANES, (p + 1) * LANES)
        q = _rope(q_ref[:, sl].astype(F32), cc, s1c, s2c)
        g = p // pairs_per_kv
        o = _pair_attention(q, k2[g], v2[g], mask, sink_ref[2 * p], sink_ref[2 * p + 1])
        o_ref[:, sl] = (o * _silu(g_ref[:, sl].astype(F32))).astype(o_ref.dtype)


def _attn_prompt(f, zskv, sinks, B, T):
    W = WINDOW
    nb = T // W
    tabs = _rope_tables(jnp.arange(T))
    kcol = SHIFT_W // KV_W
    cur = lambda b, i: (b * nb + i, 0)
    tcur = pl.BlockSpec((W, LANES), lambda b, i: (i, 0))
    tprv = pl.BlockSpec((W, LANES), lambda b, i: (jnp.maximum(i - 1, 0), 0))
    return pl.pallas_call(
        _attn_prompt_kernel,
        out_shape=(jax.ShapeDtypeStruct((B * T, ATT_W), BF16), jax.ShapeDtypeStruct((B * T, KV_W), F32)),
        grid=(B, nb),
        in_specs=[pl.BlockSpec(memory_space=pltpu.SMEM),
                  pl.BlockSpec((W, ATT_W), lambda b, i: (b * nb + i, F_Q // ATT_W)),
                  pl.BlockSpec((W, KV_W), lambda b, i: (b * nb + i, kcol)),
                  pl.BlockSpec((W, KV_W), lambda b, i: (b * nb + jnp.maximum(i - 1, 0), kcol)),
                  pl.BlockSpec((W, KV_W), lambda b, i: (b * nb + i, kcol + 1)),
                  pl.BlockSpec((W, KV_W), lambda b, i: (b * nb + jnp.maximum(i - 1, 0), kcol + 1)),
                  pl.BlockSpec((W, F_GA_BLOCK), lambda b, i: (b * nb + i, F_GA // F_GA_BLOCK)),
                  tcur, tcur, tcur, tprv, tprv, tprv],
        out_specs=(pl.BlockSpec((W, ATT_W), cur), pl.BlockSpec((W, KV_W), cur)),
        compiler_params=_params("parallel", "parallel"),
        name="attn_prompt",
    )(sinks, f, zskv, zskv, zskv, zskv, f, *tabs, *tabs)


def _attn_sample_kernel(q_ref, kn_ref, vn_ref, ck_ref, cv_ref, g_ref, sink_ref, c_ref, s1_ref, s2_ref,
                        o_ref, ko_ref, vo_ref):
    W = WINDOW
    nbat = ck_ref.shape[0]
    Ts = q_ref.shape[0] // nbat
    GQ = N_Q_HEADS // N_KV_HEADS
    GW = GQ * HEAD
    nrow = N_Q_HEADS * Ts
    tq = lax.broadcasted_iota(jnp.int32, (nrow, W + Ts), 0) % Ts
    kj = lax.broadcasted_iota(jnp.int32, (nrow, W + Ts), 1)
    rel = W + tq - kj
    mask = (rel >= 0) & (rel < W)
    sink = sink_ref[...] * LOG2E
    lane_head = lax.broadcasted_iota(jnp.int32, (Ts, GW), 1) // HEAD
    c, s1, s2 = c_ref[...], s1_ref[...], s2_ref[...]
    tiles_per_group = GW // LANES
    for bi in range(nbat):
        rows = slice(bi * Ts, (bi + 1) * Ts)
        k2, v2 = [], []
        for t in range(KV_W // LANES):
            sl = slice(t * LANES, (t + 1) * LANES)
            kcat = jnp.concatenate([ck_ref[bi, :, sl], _rope(kn_ref[rows, sl], c, s1, s2)], axis=0)
            vcat = jnp.concatenate([cv_ref[bi, :, sl], vn_ref[rows, sl]], axis=0)
            ko_ref[bi, :, sl] = kcat[Ts:, :]
            vo_ref[bi, :, sl] = vcat[Ts:, :]
            k2.extend(x.astype(BF16) for x in _dup_heads(kcat))
            v2.extend(x.astype(BF16) for x in _dup_heads(vcat))
        s_parts = []
        for g in range(N_KV_HEADS):
            qg = jnp.concatenate(
                [_rope(q_ref[rows, (g * tiles_per_group + i) * LANES:(g * tiles_per_group + i + 1) * LANES]
                       .astype(F32), c, s1, s2)
                 for i in range(tiles_per_group)], axis=1)
            qg = qg * QK_SCALE
            qs = jnp.concatenate([jnp.where(lane_head == h, qg, 0.0) for h in range(GQ)], axis=0).astype(BF16)
            k8 = jnp.concatenate([k2[g]] * tiles_per_group, axis=1)
            s_parts.append(lax.dot_general(qs, k8, (((1,), (1,)), ((), ())), preferred_element_type=F32))
        s = jnp.where(mask, jnp.concatenate(s_parts, axis=0), -1e30)
        p = _softmax_with_sink(s, sink).astype(BF16)
        for g in range(N_KV_HEADS):
            v8 = jnp.concatenate([v2[g]] * tiles_per_group, axis=1)
            o = jnp.dot(p[g * GQ * Ts:(g + 1) * GQ * Ts], v8, preferred_element_type=F32)
            og = o[0:Ts]
            for h in range(1, GQ):
                og = jnp.where(lane_head == h, o[h * Ts:(h + 1) * Ts], og)
            gl = slice(g * GW, (g + 1) * GW)
            o_ref[rows, gl] = (og * _silu(g_ref[rows, gl].astype(F32))).astype(o_ref.dtype)


def _attn_sample(f, zskv, sinks, cache_k, cache_v, B, Ts):
    W = WINDOW
    nbat = _tile(B, (4, 2, 1))
    tabs = _rope_tables(PAST_LEN + jnp.arange(Ts))
    kcol = SHIFT_W // KV_W
    rows = nbat * Ts
    sink_col = jnp.repeat(sinks, Ts).reshape(N_Q_HEADS * Ts, 1)
    cspec = pl.BlockSpec((nbat, W, KV_W), lambda i: (i, 0, 0))
    tspec = pl.BlockSpec((Ts, LANES), lambda i: (0, 0))
    cache = jax.ShapeDtypeStruct((B, W, KV_W), F32)
    return pl.pallas_call(
        _attn_sample_kernel,
        out_shape=(jax.ShapeDtypeStruct((B * Ts, ATT_W), BF16), cache, cache),
        grid=(B // nbat,),
        in_specs=[pl.BlockSpec((rows, ATT_W), lambda i: (i, F_Q // ATT_W)),
                  pl.BlockSpec((rows, KV_W), lambda i: (i, kcol)),
                  pl.BlockSpec((rows, KV_W), lambda i: (i, kcol + 1)),
                  cspec, cspec,
                  pl.BlockSpec((rows, F_GA_BLOCK), lambda i: (i, F_GA // F_GA_BLOCK)),
                  pl.BlockSpec((N_Q_HEADS * Ts, 1), lambda i: (0, 0)),
                  tspec, tspec, tspec],
        out_specs=(pl.BlockSpec((rows, ATT_W), lambda i: (i, 0)), cspec, cspec),
        compiler_params=_params("parallel"),
        name="attn_sample",
    )(f, zskv, zskv, cache_k, cache_v, f, sink_col, *tabs)


def _mix_kernel(yr_ref, ya_ref, wr_ref, wa_ref, gr_ref, ga_ref, o_ref):
    br = jnp.dot(yr_ref[...], wr_ref[...], preferred_element_type=F32)
    ba = jnp.dot(ya_ref[...], wa_ref[...], preferred_element_type=F32)
    o_ref[...] = (_sigmoid(gr_ref[...].astype(F32)) * br
                  + _sigmoid(ga_ref[...].astype(F32)) * ba).astype(o_ref.dtype)


def _mix(y_r, y_a, w_r, w_a, f):
    M = y_r.shape[0]
    D = w_r.shape[1]
    tm = _tile(M, (1024, 512, 256, 128))
    tn = _tile(D, (512, 256, 128))
    nj = D // tn
    j0 = F_M // tn
    return pl.pallas_call(
        _mix_kernel,
        out_shape=jax.ShapeDtypeStruct((M, D), BF16),
        grid=(M // tm, nj),
        in_specs=[pl.BlockSpec((tm, RWKV_W), lambda i, j: (i, 0)),
                  pl.BlockSpec((tm, ATT_W), lambda i, j: (i, 0)),
                  pl.BlockSpec((RWKV_W, tn), lambda i, j: (0, j)),
                  pl.BlockSpec((ATT_W, tn), lambda i, j: (0, j)),
                  pl.BlockSpec((tm, tn), lambda i, j: (i, j0 + j)),
                  pl.BlockSpec((tm, tn), lambda i, j: (i, j0 + nj + j))],
        out_specs=pl.BlockSpec((tm, tn), lambda i, j: (i, j)),
        compiler_params=_params("parallel", "arbitrary"),
        name="branch_mix",
    )(y_r, y_a, w_r, w_a, f, f)


def _out_kernel(mix_ref, w_ref, x_ref, g_ref, b_ref, o_ref, *, tn):
    j = pl.program_id(1)
    col = pl.multiple_of(j * tn, tn)
    out = jnp.dot(mix_ref[...], w_ref[...], preferred_element_type=F32)
    o_ref[:, pl.ds(col, tn)] = ALPHA * x_ref[...] + out

    @pl.when(j == pl.num_programs(1) - 1)
    def _():
        h = o_ref[...]
        m = jnp.mean(h, axis=1, keepdims=True)
        d = h - m
        var = jnp.mean(d * d, axis=1, keepdims=True)
        o_ref[...] = d * lax.rsqrt(var + LN_EPS) * g_ref[...] + b_ref[...]


def _out_ln(mix, w_out, x, ln_g, ln_b):
    M, D = x.shape
    tm = _tile(M, (512, 256, 128))
    tn = _tile(D, (512, 256, 128))
    row = pl.BlockSpec((1, D), lambda i, j: (0, 0))
    return pl.pallas_call(
        functools.partial(_out_kernel, tn=tn),
        out_shape=jax.ShapeDtypeStruct((M, D), F32),
        grid=(M // tm, D // tn),
        in_specs=[pl.BlockSpec((tm, D), lambda i, j: (i, 0)),
                  pl.BlockSpec((D, tn), lambda i, j: (0, j)),
                  pl.BlockSpec((tm, tn), lambda i, j: (i, j)),
                  row, row],
        out_specs=pl.BlockSpec((tm, D), lambda i, j: (i, 0)),
        compiler_params=_params("parallel", "arbitrary"),
        name="out_ln",
    )(mix, w_out, x, ln_g, ln_b)


def _k_major(x, axis):
    shp = x.shape
    n = shp[axis] // RWKV_W
    y = x.reshape(shp[:axis] + (n, RWKV_HEADS, HEAD) + shp[axis + 1:])
    y = jnp.swapaxes(y, axis + 1, axis + 2)
    return y.reshape(shp)


def _head_major(x, axis):
    shp = x.shape
    n = shp[axis] // RWKV_W
    y = x.reshape(shp[:axis] + (n, HEAD, RWKV_HEADS) + shp[axis + 1:])
    y = jnp.swapaxes(y, axis + 1, axis + 2)
    return y.reshape(shp)


def _shift_k_major(x):
    return jnp.concatenate([_k_major(x[..., :3 * RWKV_W], x.ndim - 1), x[..., 3 * RWKV_W:]], axis=-1)


def _shift_head_major(x):
    return jnp.concatenate([_head_major(x[..., :3 * RWKV_W], x.ndim - 1), x[..., 3 * RWKV_W:]], axis=-1)


def _layer(x, S0, shift0, caches, w, p):
    B, T, D = x.shape
    M = B * T
    G = B // QUAD
    x2 = x.reshape(M, D)
    xb = x2.astype(BF16)
    zskv = _matmul(xb, w["zskv"], F32, name="proj_shift_kv")
    zskv3 = zskv.reshape(B, T, ZSKV_W)
    prep = _rwkv_prep(zskv3, _shift_k_major(shift0).reshape(B, 1, SHIFT_W), p)
    ops = [a.reshape(G, T, HEAD, LANES) for a in prep[:6]]
    s0 = S0.transpose(3, 2, 0, 1).reshape(HEAD, HEAD, B * RWKV_HEADS)
    f = _matmul_cols(xb, w["all"], W_Q, BF16, name="proj_q_gates")
    g_r = _matmul(xb, w["g_r"], BF16, name="proj_gate_r")
    yn, s_fin = _wkv_scan(ops, s0)
    y_r = _rwkv_gate(yn, prep[6], g_r.reshape(B, T, RWKV_W), p["gn_w"], p["gn_b"]).reshape(M, RWKV_W)
    S_new = s_fin.reshape(HEAD, HEAD, B, RWKV_HEADS).transpose(2, 3, 1, 0)
    shift_new = _shift_head_major(zskv3[:, T - 1, :SHIFT_W])

    if caches is None:
        y_a, k_rot = _attn_prompt(f, zskv, p["sinks"], B, T)
        k_buf = k_rot.reshape(B, T, N_KV_HEADS, HEAD)[:, T - WINDOW:]
        v_buf = zskv3[:, T - WINDOW:, SHIFT_W + KV_W:].reshape(B, WINDOW, N_KV_HEADS, HEAD)
    else:
        y_a, k_buf, v_buf = _attn_sample(f, zskv, p["sinks"], caches[0], caches[1], B, T)
        k_buf = k_buf.reshape(B, WINDOW, N_KV_HEADS, HEAD)
        v_buf = v_buf.reshape(B, WINDOW, N_KV_HEADS, HEAD)

    mix = _mix(y_r, y_a, w["branch_r"], w["branch_a"], f)
    y = _out_ln(mix, w["out"], x2, p["ln_g"], p["ln_b"]).reshape(B, T, D)
    return y, S_new, shift_new, k_buf, v_buf


def kernel(x_prompt, x_sample, state_wkv, state_shift, cache_win_k, cache_win_v, w_in, mu_shift, w0, w_decay_up, a0, w_aaa_up, k_k, k_a, r_k, gn_w, gn_b, sinks, w_branch_rwkv, w_branch_attn, w_out, ln_g, ln_b):
    assert w_in.shape[0] == DEPTH
    Bp = x_prompt.shape[0]
    Bs = x_sample.shape[0]
    assert Bp % QUAD == 0 and Bs % QUAD == 0
    o_gr = SHIFT_W
    o_k = W_Q + ATT_W
    o_ga = o_k + 2 * KV_W
    wi = w_in[0].astype(BF16)
    w = {"all": wi,
         "zskv": jnp.concatenate([_shift_k_major(wi[:, :o_gr]), wi[:, o_k:o_ga]], axis=1),
         "g_r": _k_major(wi[:, o_gr:W_Q], 1),
         "branch_r": _k_major(w_branch_rwkv[0], 0).astype(BF16),
         "branch_a": w_branch_attn[0].astype(BF16), "out": w_out[0].astype(BF16)}
    lane = jnp.arange(LANES)
    km = lambda a: _k_major(a.reshape(1, RWKV_W), 1)
    p = {"mu": _shift_k_major(mu_shift), "w0": km(w0), "wup": _k_major(w_decay_up[0], 1).astype(BF16),
         "a0": km(a0), "aup": _k_major(w_aaa_up[0], 1).astype(BF16), "k_k": km(k_k), "k_a": km(k_a),
         "r_k": km(r_k), "gn_w": km(gn_w), "gn_b": km(gn_b), "sinks": sinks[0], "ln_g": ln_g, "ln_b": ln_b,
         "jmat": (lane[:, None] % RWKV_HEADS == lane[None, :] % RWKV_HEADS).astype(BF16)}

    S0p = jnp.zeros((Bp, RWKV_HEADS, HEAD, HEAD), F32)
    sh0p = jnp.zeros((Bp, SHIFT_W), F32)
    yp, a1, a2, a3, a4 = _layer(x_prompt, S0p, sh0p, None, w, p)
    caches = (cache_win_k[0].reshape(Bs, WINDOW, KV_W), cache_win_v[0].reshape(Bs, WINDOW, KV_W))
    ys, b1, b2, b3, b4 = _layer(x_sample, state_wkv[0], state_shift[0], caches, w, p)
    return (yp, ys, a1[None], a2[None], a3[None], a4[None], b1[None], b2[None], b3[None], b4[None])
```

```python
import functools
import math

import jax
import jax.numpy as jnp
from jax import lax
from jax.experimental import pallas as pl
from jax.experimental.pallas import tpu as pltpu

RWKV_HEADS = 32
HEAD = 64
RWKV_W = RWKV_HEADS * HEAD
LORA = 128
SHIFT_W = 3 * RWKV_W + 2 * LORA
GN_EPS = 64e-5
N_Q_HEADS = 32
N_KV_HEADS = 4
ATT_W = N_Q_HEADS * HEAD
KV_W = N_KV_HEADS * HEAD
WINDOW = 128
ROT_DIM = HEAD // 4
ROPE_THETA = 500000.0
PAST_LEN = 8192
DEPTH = 1
ALPHA = (2.0 * DEPTH) ** 0.25
LN_EPS = 1e-5
LANES = 128
QUAD = LANES // RWKV_HEADS
DECAY_SCALE = math.exp(-0.5)
LOG2E = math.log2(math.e)
QK_SCALE = LOG2E / math.sqrt(HEAD)
ZSKV_W = SHIFT_W + 2 * KV_W
W_Q = SHIFT_W + RWKV_W
F_Q = 0
F_GA = ATT_W + 2 * KV_W
F_M = F_GA + ATT_W
F_GA_BLOCK = F_GA
W_TILE = 256
VMEM_LIMIT = 56 * 1024 * 1024

BF16 = jnp.bfloat16
F32 = jnp.float32


def _params(*sem):
    return pltpu.CompilerParams(dimension_semantics=sem, vmem_limit_bytes=VMEM_LIMIT)


def _tile(n, prefs):
    for p in prefs:
        if n % p == 0:
            return p
    return n


def _sigmoid(x):
    return 1.0 / (1.0 + jnp.exp(-x))


def _silu(x):
    return x * _sigmoid(x)


def _mm_kernel(x_ref, w_ref, o_ref):
    o_ref[...] = jnp.dot(x_ref[...], w_ref[...], preferred_element_type=F32).astype(o_ref.dtype)


def _matmul(x, w, out_dtype, name="proj"):
    M, K = x.shape
    N = w.shape[1]
    tm = _tile(M, (1024, 512, 256, 128))
    tn = _tile(N, (512, 768, 640, 256, 128))
    return pl.pallas_call(
        _mm_kernel,
        out_shape=jax.ShapeDtypeStruct((M, N), out_dtype),
        grid=(M // tm, N // tn),
        in_specs=[pl.BlockSpec((tm, K), lambda i, j: (i, 0)),
                  pl.BlockSpec((K, tn), lambda i, j: (0, j))],
        out_specs=pl.BlockSpec((tm, tn), lambda i, j: (i, j)),
        compiler_params=_params("parallel", "arbitrary"),
        name=name,
    )(x, w)


def _mm2_kernel(x_ref, wa_ref, wb_ref, o_ref):
    x = x_ref[...]
    o_ref[:, :W_TILE] = jnp.dot(x, wa_ref[...], preferred_element_type=F32).astype(o_ref.dtype)
    o_ref[:, W_TILE:] = jnp.dot(x, wb_ref[...], preferred_element_type=F32).astype(o_ref.dtype)


def _matmul_cols(x, w, col0, out_dtype, name="proj"):
    M, K = x.shape
    N = w.shape[1] - col0
    assert col0 % W_TILE == 0 and N % (2 * W_TILE) == 0
    tm = _tile(M, (1024, 512, 256, 128))
    j0 = col0 // W_TILE
    return pl.pallas_call(
        _mm2_kernel,
        out_shape=jax.ShapeDtypeStruct((M, N), out_dtype),
        grid=(M // tm, N // (2 * W_TILE)),
        in_specs=[pl.BlockSpec((tm, K), lambda i, j: (i, 0)),
                  pl.BlockSpec((K, W_TILE), lambda i, j: (0, j0 + 2 * j)),
                  pl.BlockSpec((K, W_TILE), lambda i, j: (0, j0 + 2 * j + 1))],
        out_specs=pl.BlockSpec((tm, 2 * W_TILE), lambda i, j: (i, j)),
        compiler_params=_params("parallel", "arbitrary"),
        name=name,
    )(x, w, w)


def _head_sum(x, j_ref):
    nt = x.shape[1] // LANES
    t = x[:, 0:LANES]
    for m in range(1, nt):
        t = t + x[:, m * LANES:(m + 1) * LANES]
    hi = t.astype(BF16)
    lo = (t - hi.astype(F32)).astype(BF16)
    s = (jnp.dot(hi, j_ref[...], preferred_element_type=F32)
         + jnp.dot(lo, j_ref[...], preferred_element_type=F32))
    return jnp.concatenate([s] * nt, axis=1)


def _lane_quarter_select(parts):
    lane = lax.broadcasted_iota(jnp.int32, parts[0].shape, 1)
    out = parts[QUAD - 1]
    for q in range(QUAD - 2, -1, -1):
        out = jnp.where(lane < (q + 1) * RWKV_HEADS, parts[q], out)
    return out


def _roll_lanes(x, shift):
    shift = shift % LANES
    return x if shift == 0 else pltpu.roll(x, shift, axis=1)


def _quad_transpose(tiles):
    rolled = []
    for s in range(QUAD):
        gathered = _lane_quarter_select([tiles[(g + s) % QUAD] for g in range(QUAD)])
        rolled.append(_roll_lanes(gathered, s * RWKV_HEADS))
    return [_lane_quarter_select([rolled[(a - c) % QUAD] for a in range(QUAD)]) for c in range(QUAD)]


def _interleave_store(xs, o_ref):
    for m in range(RWKV_W // LANES):
        outs = _quad_transpose([x[:, m * LANES:(m + 1) * LANES] for x in xs])
        for j in range(QUAD):
            k = m * QUAD + j
            o_ref[0, :, k * LANES:(k + 1) * LANES] = outs[j]


def _prep_kernel(zs_ref, sh_ref, mu_ref, w0_ref, wup_ref, a0_ref, aup_ref, kk_ref, ka_ref, rk_ref, j_ref,
                 r_o, w_o, kap_o, b_o, kt_o, v_o, bon_o, carry):
    @pl.when(pl.program_id(1) == 0)
    def _():
        carry[...] = sh_ref[...]

    tb = zs_ref.shape[1]
    row = lax.broadcasted_iota(jnp.int32, (tb, 1), 0)
    outs = [[] for _ in range(6)]
    for bq in range(QUAD):
        zs = zs_ref[bq]
        rolled = pltpu.roll(zs, 1, axis=0)
        prev = jnp.where(row == 0, carry[bq], rolled)
        carry[bq] = zs[tb - 1:tb, :]
        xs = zs + mu_ref[...] * (prev - zs)
        r = xs[:, 0:RWKV_W]
        k = xs[:, RWKV_W:2 * RWKV_W]
        v = xs[:, 2 * RWKV_W:3 * RWKV_W]
        wd = xs[:, 3 * RWKV_W:3 * RWKV_W + LORA]
        ad = xs[:, 3 * RWKV_W + LORA:SHIFT_W]
        wl = w0_ref[...] + jnp.dot(jnp.tanh(wd).astype(BF16), wup_ref[...], preferred_element_type=F32)
        dec = jnp.exp(-DECAY_SCALE * _sigmoid(wl))
        a = _sigmoid(a0_ref[...] + jnp.dot(ad.astype(BF16), aup_ref[...], preferred_element_type=F32))
        kkv = k * kk_ref[...]
        norm = jnp.maximum(jnp.sqrt(_head_sum(kkv * kkv, j_ref)), 1e-12)
        kap = kkv / norm
        kt = k * (1.0 + (a - 1.0) * ka_ref[...])
        bon_o[bq] = _head_sum(r * kt * rk_ref[...], j_ref) * v
        for lst, val in zip(outs, (r, dec, kap, kap * a, kt, v)):
            lst.append(val)
    for lst, o_ref in zip(outs, (r_o, w_o, kap_o, b_o, kt_o, v_o)):
        _interleave_store(lst, o_ref)


def _rwkv_prep(zs, shift0, p):
    B, T, _ = zs.shape
    G = B // QUAD
    tb = _tile(T, (32, 16, 8))
    row = lambda n: pl.BlockSpec((1, n), lambda g, i: (0, 0))
    full = lambda a: pl.BlockSpec(a.shape, lambda g, i: (0, 0))
    op = jax.ShapeDtypeStruct((G, T, HEAD * LANES), F32)
    opspec = pl.BlockSpec((1, tb, HEAD * LANES), lambda g, i: (g, i, 0))
    return pl.pallas_call(
        _prep_kernel,
        out_shape=(op,) * 6 + (jax.ShapeDtypeStruct((B, T, RWKV_W), F32),),
        grid=(G, T // tb),
        in_specs=[pl.BlockSpec((QUAD, tb, SHIFT_W), lambda g, i: (g, i, 0)),
                  pl.BlockSpec((QUAD, 1, SHIFT_W), lambda g, i: (g, 0, 0)),
                  row(SHIFT_W), row(RWKV_W), full(p["wup"]), row(RWKV_W), full(p["aup"]),
                  row(RWKV_W), row(RWKV_W), row(RWKV_W), full(p["jmat"])],
        out_specs=(opspec,) * 6 + (pl.BlockSpec((QUAD, tb, RWKV_W), lambda g, i: (g, i, 0)),),
        scratch_shapes=[pltpu.VMEM((QUAD, 1, SHIFT_W), F32)],
        compiler_params=_params("parallel", "arbitrary"),
        name="rwkv_prep",
    )(zs, shift0, p["mu"], p["w0"], p["wup"], p["a0"], p["aup"], p["k_k"], p["k_a"], p["r_k"], p["jmat"])


def _scan_kernel(r_ref, w_ref, kap_ref, b_ref, kt_ref, v_ref, s0_ref, y_ref, sout_ref, S):
    ti = pl.program_id(1)

    @pl.when(ti == 0)
    def _():
        S[...] = s0_ref[...]

    tb = r_ref.shape[1]

    u0 = S[0] * kap_ref[0, 0, 0:1, :]
    for k in range(1, HEAD):
        u0 = u0 + S[k] * kap_ref[0, 0, k:k + 1, :]

    def group_norm(y):
        m = jnp.mean(y, axis=0, keepdims=True)
        d = y - m
        var = jnp.mean(d * d, axis=0, keepdims=True)
        return d * lax.rsqrt(var + GN_EPS)

    def step(t, carry):
        u, y_prev = carry
        y_ref[0, jnp.maximum(t - 1, 0)] = group_norm(y_prev)
        tn = jnp.minimum(t + 1, tb - 1)
        vt = v_ref[0, t]
        y = None
        un = None
        for k in range(HEAD):
            s = S[k] * w_ref[0, t, k:k + 1, :] - u * b_ref[0, t, k:k + 1, :] + vt * kt_ref[0, t, k:k + 1, :]
            S[k] = s
            ty = s * r_ref[0, t, k:k + 1, :]
            tu = s * kap_ref[0, tn, k:k + 1, :]
            y = ty if y is None else y + ty
            un = tu if un is None else un + tu
        return un, y

    _, y_last = lax.fori_loop(0, tb, step, (u0, jnp.zeros((HEAD, LANES), F32)))
    y_ref[0, tb - 1] = group_norm(y_last)

    @pl.when(ti == pl.num_programs(1) - 1)
    def _():
        sout_ref[...] = S[...]


def _wkv_scan(ops, s0):
    G, T = ops[0].shape[:2]
    tb = _tile(T, (32, 16, 8))
    ospec = pl.BlockSpec((1, tb, HEAD, LANES), lambda g, i: (g, i, 0, 0))
    sspec = pl.BlockSpec((HEAD, HEAD, LANES), lambda g, i: (0, 0, g))
    return pl.pallas_call(
        _scan_kernel,
        out_shape=(jax.ShapeDtypeStruct((G, T, HEAD, LANES), F32),
                   jax.ShapeDtypeStruct((HEAD, HEAD, G * LANES), F32)),
        grid=(G, T // tb),
        in_specs=[ospec] * 6 + [sspec],
        out_specs=(ospec, sspec),
        scratch_shapes=[pltpu.VMEM((HEAD, HEAD, LANES), F32)],
        compiler_params=_params("parallel", "arbitrary"),
        name="wkv_scan",
    )(*ops, s0)


def _gate_kernel(yn_ref, bon_ref, g_ref, gw_ref, gb_ref, o_ref):
    for m in range(RWKV_W // LANES):
        yns = _quad_transpose([yn_ref[0, :, m * QUAD + j, :] for j in range(QUAD)])
        sl = slice(m * LANES, (m + 1) * LANES)
        for bq in range(QUAD):
            y = yns[bq] * gw_ref[:, sl] + gb_ref[:, sl] + bon_ref[bq, :, sl]
            o_ref[bq, :, sl] = (y * _silu(g_ref[bq, :, sl].astype(F32))).astype(o_ref.dtype)


def _rwkv_gate(yn, bonus, g, gn_w, gn_b):
    B, T, _ = bonus.shape
    G = B // QUAD
    tb = _tile(T, (128, 64, 32, 16, 8))
    spec = pl.BlockSpec((QUAD, tb, RWKV_W), lambda g_, i: (g_, i, 0))
    row = pl.BlockSpec((1, RWKV_W), lambda g_, i: (0, 0))
    return pl.pallas_call(
        _gate_kernel,
        out_shape=jax.ShapeDtypeStruct((B, T, RWKV_W), BF16),
        grid=(G, T // tb),
        in_specs=[pl.BlockSpec((1, tb, HEAD, LANES), lambda g_, i: (g_, i, 0, 0)), spec, spec, row, row],
        out_specs=spec,
        compiler_params=_params("parallel", "parallel"),
        name="rwkv_gate",
    )(yn, bonus, g, gn_w, gn_b)


def _rope_tables(pos):
    half = ROT_DIM // 2
    inv = ROPE_THETA ** (-jnp.arange(half, dtype=F32) * 2.0 / ROT_DIM)
    ang = pos.astype(F32)[:, None] * inv[None, :]
    cos, sin = jnp.cos(ang), jnp.sin(ang)
    n = pos.shape[0]
    one = jnp.ones((n, HEAD - ROT_DIM), F32)
    zero = jnp.zeros((n, HEAD - ROT_DIM), F32)
    zh = jnp.zeros((n, half), F32)
    c = jnp.concatenate([cos, cos, one], axis=1)
    s1 = jnp.concatenate([zh, sin, zero], axis=1)
    s2 = jnp.concatenate([-sin, zh, zero], axis=1)
    reps = LANES // HEAD
    return jnp.tile(c, (1, reps)), jnp.tile(s1, (1, reps)), jnp.tile(s2, (1, reps))


def _rope(x, c, s1, s2):
    half = ROT_DIM // 2
    return x * c + pltpu.roll(x, half, axis=1) * s1 + pltpu.roll(x, LANES - half, axis=1) * s2


def _dup_heads(x):
    sw = pltpu.roll(x, HEAD, axis=1)
    low = lax.broadcasted_iota(jnp.int32, x.shape, 1) < HEAD
    return jnp.where(low, x, sw), jnp.where(low, sw, x)


def _softmax_with_sink(s, sink):
    m = jnp.maximum(jnp.max(s, axis=1, keepdims=True), sink)
    e = jnp.exp2(s - m)
    den = jnp.sum(e, axis=1, keepdims=True) + jnp.exp2(sink - m)
    return e * (1.0 / den)


def _pair_attention(q, k2, v2, mask, sink0, sink1):
    n = q.shape[0]
    low = lax.broadcasted_iota(jnp.int32, q.shape, 1) < HEAD
    q = q * QK_SCALE
    qs = jnp.concatenate([jnp.where(low, q, 0.0), jnp.where(low, 0.0, q)], axis=0).astype(BF16)
    s = lax.dot_general(qs, k2, (((1,), (1,)), ((), ())), preferred_element_type=F32)
    s = jnp.where(mask, s, -1e30)
    top = lax.broadcasted_iota(jnp.int32, (2 * n, 1), 0) < n
    sink = jnp.where(top, sink0, sink1) * LOG2E
    p = _softmax_with_sink(s, sink).astype(BF16)
    o = jnp.dot(p, v2, preferred_element_type=F32)
    return jnp.where(low, o[:n], o[n:])


def _attn_prompt_kernel(sink_ref, q_ref, kc_ref, kp_ref, vc_ref, vp_ref, g_ref,
                        cc_ref, s1c_ref, s2c_ref, cp_ref, s1p_ref, s2p_ref, o_ref, krot_ref):
    i = pl.program_id(1)
    W = WINDOW
    qi = lax.broadcasted_iota(jnp.int32, (2 * W, 2 * W), 0) % W
    kj = lax.broadcasted_iota(jnp.int32, (2 * W, 2 * W), 1)
    rel = W + qi - kj
    mask = (rel >= 0) & (rel < W) & ((i > 0) | (kj >= W))
    cc, s1c, s2c = cc_ref[...], s1c_ref[...], s2c_ref[...]
    cp, s1p, s2p = cp_ref[...], s1p_ref[...], s2p_ref[...]
    k2, v2 = [], []
    for t in range(KV_W // LANES):
        sl = slice(t * LANES, (t + 1) * LANES)
        kc = _rope(kc_ref[:, sl], cc, s1c, s2c)
        krot_ref[:, sl] = kc
        kcat = jnp.concatenate([_rope(kp_ref[:, sl], cp, s1p, s2p), kc], axis=0)
        vcat = jnp.concatenate([vp_ref[:, sl], vc_ref[:, sl]], axis=0)
        k2.extend(x.astype(BF16) for x in _dup_heads(kcat))
        v2.extend(x.astype(BF16) for x in _dup_heads(vcat))
    pairs_per_kv = N_Q_HEADS // N_KV_HEADS // 2
    for p in range(N_Q_HEADS // 2):
        sl = slice(p * LANES, (p + 1) * LANES)
        q = _rope(q_ref[:, sl].astype(F32), cc, s1c, s2c)
        g = p // pairs_per_kv
        o = _pair_attention(q, k2[g], v2[g], mask, sink_ref[2 * p], sink_ref[2 * p + 1])
        o_ref[:, sl] = (o * _silu(g_ref[:, sl].astype(F32))).astype(o_ref.dtype)


def _attn_prompt(f, zskv, sinks, B, T):
    W = WINDOW
    nb = T // W
    tabs = _rope_tables(jnp.arange(T))
    kcol = SHIFT_W // KV_W
    cur = lambda b, i: (b * nb + i, 0)
    tcur = pl.BlockSpec((W, LANES), lambda b, i: (i, 0))
    tprv = pl.BlockSpec((W, LANES), lambda b, i: (jnp.maximum(i - 1, 0), 0))
    return pl.pallas_call(
        _attn_prompt_kernel,
        out_shape=(jax.ShapeDtypeStruct((B * T, ATT_W), BF16), jax.ShapeDtypeStruct((B * T, KV_W), F32)),
        grid=(B, nb),
        in_specs=[pl.BlockSpec(memory_space=pltpu.SMEM),
                  pl.BlockSpec((W, ATT_W), lambda b, i: (b * nb + i, F_Q // ATT_W)),
                  pl.BlockSpec((W, KV_W), lambda b, i: (b * nb + i, kcol)),
                  pl.BlockSpec((W, KV_W), lambda b, i: (b * nb + jnp.maximum(i - 1, 0), kcol)),
                  pl.BlockSpec((W, KV_W), lambda b, i: (b * nb + i, kcol + 1)),
                  pl.BlockSpec((W, KV_W), lambda b, i: (b * nb + jnp.maximum(i - 1, 0), kcol + 1)),
                  pl.BlockSpec((W, F_GA_BLOCK), lambda b, i: (b * nb + i, F_GA // F_GA_BLOCK)),
                  tcur, tcur, tcur, tprv, tprv, tprv],
        out_specs=(pl.BlockSpec((W, ATT_W), cur), pl.BlockSpec((W, KV_W), cur)),
        compiler_params=_params("parallel", "parallel"),
        name="attn_prompt",
    )(sinks, f, zskv, zskv, zskv, zskv, f, *tabs, *tabs)


def _attn_sample_kernel(q_ref, kn_ref, vn_ref, ck_ref, cv_ref, g_ref, sink_ref, c_ref, s1_ref, s2_ref,
                        o_ref, ko_ref, vo_ref):
    W = WINDOW
    nbat = ck_ref.shape[0]
    Ts = q_ref.shape[0] // nbat
    GQ = N_Q_HEADS // N_KV_HEADS
    GW = GQ * HEAD
    nrow = N_Q_HEADS * Ts
    tq = lax.broadcasted_iota(jnp.int32, (nrow, W + Ts), 0) % Ts
    kj = lax.broadcasted_iota(jnp.int32, (nrow, W + Ts), 1)
    rel = W + tq - kj
    mask = (rel >= 0) & (rel < W)
    sink = sink_ref[...] * LOG2E
    lane_head = lax.broadcasted_iota(jnp.int32, (Ts, GW), 1) // HEAD
    c, s1, s2 = c_ref[...], s1_ref[...], s2_ref[...]
    tiles_per_group = GW // LANES
    for bi in range(nbat):
        rows = slice(bi * Ts, (bi + 1) * Ts)
        k2, v2 = [], []
        for t in range(KV_W // LANES):
            sl = slice(t * LANES, (t + 1) * LANES)
            kcat = jnp.concatenate([ck_ref[bi, :, sl], _rope(kn_ref[rows, sl], c, s1, s2)], axis=0)
            vcat = jnp.concatenate([cv_ref[bi, :, sl], vn_ref[rows, sl]], axis=0)
            ko_ref[bi, :, sl] = kcat[Ts:, :]
            vo_ref[bi, :, sl] = vcat[Ts:, :]
            k2.extend(x.astype(BF16) for x in _dup_heads(kcat))
            v2.extend(x.astype(BF16) for x in _dup_heads(vcat))
        s_parts = []
        for g in range(N_KV_HEADS):
            qg = jnp.concatenate(
                [_rope(q_ref[rows, (g * tiles_per_group + i) * LANES:(g * tiles_per_group + i + 1) * LANES]
                       .astype(F32), c, s1, s2)
                 for i in range(tiles_per_group)], axis=1)
            qg = qg * QK_SCALE
            qs = jnp.concatenate([jnp.where(lane_head == h, qg, 0.0) for h in range(GQ)], axis=0).astype(BF16)
            k8 = jnp.concatenate([k2[g]] * tiles_per_group, axis=1)
            s_parts.append(lax.dot_general(qs, k8, (((1,), (1,)), ((), ())), preferred_element_type=F32))
        s = jnp.where(mask, jnp.concatenate(s_parts, axis=0), -1e30)
        p = _softmax_with_sink(s, sink).astype(BF16)
        for g in range(N_KV_HEADS):
            v8 = jnp.concatenate([v2[g]] * tiles_per_group, axis=1)
            o = jnp.dot(p[g * GQ * Ts:(g + 1) * GQ * Ts], v8, preferred_element_type=F32)
            og = o[0:Ts]
            for h in range(1, GQ):
                og = jnp.where(lane_head == h, o[h * Ts:(h + 1) * Ts], og)
            gl = slice(g * GW, (g + 1) * GW)
            o_ref[rows, gl] = (og * _silu(g_ref[rows, gl].astype(F32))).astype(o_ref.dtype)


def _attn_sample(f, zskv, sinks, cache_k, cache_v, B, Ts):
    W = WINDOW
    nbat = _tile(B, (4, 2, 1))
    tabs = _rope_tables(PAST_LEN + jnp.arange(Ts))
    kcol = SHIFT_W // KV_W
    rows = nbat * Ts
    sink_col = jnp.repeat(sinks, Ts).reshape(N_Q_HEADS * Ts, 1)
    cspec = pl.BlockSpec((nbat, W, KV_W), lambda i: (i, 0, 0))
    tspec = pl.BlockSpec((Ts, LANES), lambda i: (0, 0))
    cache = jax.ShapeDtypeStruct((B, W, KV_W), F32)
    return pl.pallas_call(
        _attn_sample_kernel,
        out_shape=(jax.ShapeDtypeStruct((B * Ts, ATT_W), BF16), cache, cache),
        grid=(B // nbat,),
        in_specs=[pl.BlockSpec((rows, ATT_W), lambda i: (i, F_Q // ATT_W)),
                  pl.BlockSpec((rows, KV_W), lambda i: (i, kcol)),
                  pl.BlockSpec((rows, KV_W), lambda i: (i, kcol + 1)),
                  cspec, cspec,
                  pl.BlockSpec((rows, F_GA_BLOCK), lambda i: (i, F_GA // F_GA_BLOCK)),
                  pl.BlockSpec((N_Q_HEADS * Ts, 1), lambda i: (0, 0)),
                  tspec, tspec, tspec],
        out_specs=(pl.BlockSpec((rows, ATT_W), lambda i: (i, 0)), cspec, cspec),
        compiler_params=_params("parallel"),
        name="attn_sample",
    )(f, zskv, zskv, cache_k, cache_v, f, sink_col, *tabs)


def _mix_kernel(yr_ref, ya_ref, wr_ref, wa_ref, gr_ref, ga_ref, o_ref):
    br = jnp.dot(yr_ref[...], wr_ref[...], preferred_element_type=F32)
    ba = jnp.dot(ya_ref[...], wa_ref[...], preferred_element_type=F32)
    o_ref[...] = (_sigmoid(gr_ref[...].astype(F32)) * br
                  + _sigmoid(ga_ref[...].astype(F32)) * ba).astype(o_ref.dtype)


def _mix(y_r, y_a, w_r, w_a, f):
    M = y_r.shape[0]
    D = w_r.shape[1]
    tm = _tile(M, (1024, 512, 256, 128))
    tn = _tile(D, (512, 256, 128))
    nj = D // tn
    j0 = F_M // tn
    return pl.pallas_call(
        _mix_kernel,
        out_shape=jax.ShapeDtypeStruct((M, D), BF16),
        grid=(M // tm, nj),
        in_specs=[pl.BlockSpec((tm, RWKV_W), lambda i, j: (i, 0)),
                  pl.BlockSpec((tm, ATT_W), lambda i, j: (i, 0)),
                  pl.BlockSpec((RWKV_W, tn), lambda i, j: (0, j)),
                  pl.BlockSpec((ATT_W, tn), lambda i, j: (0, j)),
                  pl.BlockSpec((tm, tn), lambda i, j: (i, j0 + j)),
                  pl.BlockSpec((tm, tn), lambda i, j: (i, j0 + nj + j))],
        out_specs=pl.BlockSpec((tm, tn), lambda i, j: (i, j)),
        compiler_params=_params("parallel", "arbitrary"),
        name="branch_mix",
    )(y_r, y_a, w_r, w_a, f, f)


def _out_kernel(mix_ref, w_ref, x_ref, g_ref, b_ref, o_ref, *, tn):
    j = pl.program_id(1)
    col = pl.multiple_of(j * tn, tn)
    out = jnp.dot(mix_ref[...], w_ref[...], preferred_element_type=F32)
    o_ref[:, pl.ds(col, tn)] = ALPHA * x_ref[...] + out

    @pl.when(j == pl.num_programs(1) - 1)
    def _():
        h = o_ref[...]
        m = jnp.mean(h, axis=1, keepdims=True)
        d = h - m
        var = jnp.mean(d * d, axis=1, keepdims=True)
        o_ref[...] = d * lax.rsqrt(var + LN_EPS) * g_ref[...] + b_ref[...]


def _out_ln(mix, w_out, x, ln_g, ln_b):
    M, D = x.shape
    tm = _tile(M, (512, 256, 128))
    tn = _tile(D, (512, 256, 128))
    row = pl.BlockSpec((1, D), lambda i, j: (0, 0))
    return pl.pallas_call(
        functools.partial(_out_kernel, tn=tn),
        out_shape=jax.ShapeDtypeStruct((M, D), F32),
        grid=(M // tm, D // tn),
        in_specs=[pl.BlockSpec((tm, D), lambda i, j: (i, 0)),
                  pl.BlockSpec((D, tn), lambda i, j: (0, j)),
                  pl.BlockSpec((tm, tn), lambda i, j: (i, j)),
                  row, row],
        out_specs=pl.BlockSpec((tm, D), lambda i, j: (i, 0)),
        compiler_params=_params("parallel", "arbitrary"),
        name="out_ln",
    )(mix, w_out, x, ln_g, ln_b)


def _k_major(x, axis):
    shp = x.shape
    n = shp[axis] // RWKV_W
    y = x.reshape(shp[:axis] + (n, RWKV_HEADS, HEAD) + shp[axis + 1:])
    y = jnp.swapaxes(y, axis + 1, axis + 2)
    return y.reshape(shp)


def _head_major(x, axis):
    shp = x.shape
    n = shp[axis] // RWKV_W
    y = x.reshape(shp[:axis] + (n, HEAD, RWKV_HEADS) + shp[axis + 1:])
    y = jnp.swapaxes(y, axis + 1, axis + 2)
    return y.reshape(shp)


def _shift_k_major(x):
    return jnp.concatenate([_k_major(x[..., :3 * RWKV_W], x.ndim - 1), x[..., 3 * RWKV_W:]], axis=-1)


def _shift_head_major(x):
    return jnp.concatenate([_head_major(x[..., :3 * RWKV_W], x.ndim - 1), x[..., 3 * RWKV_W:]], axis=-1)


def _layer(x, S0, shift0, caches, w, p):
    B, T, D = x.shape
    M = B * T
    G = B // QUAD
    x2 = x.reshape(M, D)
    xb = x2.astype(BF16)
    zskv = _matmul(xb, w["zskv"], F32, name="proj_shift_kv")
    zskv3 = zskv.reshape(B, T, ZSKV_W)
    prep = _rwkv_prep(zskv3, _shift_k_major(shift0).reshape(B, 1, SHIFT_W), p)
    ops = [a.reshape(G, T, HEAD, LANES) for a in prep[:6]]
    s0 = S0.transpose(3, 2, 0, 1).reshape(HEAD, HEAD, B * RWKV_HEADS)
    f = _matmul_cols(xb, w["all"], W_Q, BF16, name="proj_q_gates")
    g_r = _matmul(xb, w["g_r"], BF16, name="proj_gate_r")
    yn, s_fin = _wkv_scan(ops, s0)
    y_r = _rwkv_gate(yn, prep[6], g_r.reshape(B, T, RWKV_W), p["gn_w"], p["gn_b"]).reshape(M, RWKV_W)
    S_new = s_fin.reshape(HEAD, HEAD, B, RWKV_HEADS).transpose(2, 3, 1, 0)
    shift_new = _shift_head_major(zskv3[:, T - 1, :SHIFT_W])

    if caches is None:
        y_a, k_rot = _attn_prompt(f, zskv, p["sinks"], B, T)
        k_buf = k_rot.reshape(B, T, N_KV_HEADS, HEAD)[:, T - WINDOW:]
        v_buf = zskv3[:, T - WINDOW:, SHIFT_W + KV_W:].reshape(B, WINDOW, N_KV_HEADS, HEAD)
    else:
        y_a, k_buf, v_buf = _attn_sample(f, zskv, p["sinks"], caches[0], caches[1], B, T)
        k_buf = k_buf.reshape(B, WINDOW, N_KV_HEADS, HEAD)
        v_buf = v_buf.reshape(B, WINDOW, N_KV_HEADS, HEAD)

    mix = _mix(y_r, y_a, w["branch_r"], w["branch_a"], f)
    y = _out_ln(mix, w["out"], x2, p["ln_g"], p["ln_b"]).reshape(B, T, D)
    return y, S_new, shift_new, k_buf, v_buf


def kernel(x_prompt, x_sample, state_wkv, state_shift, cache_win_k, cache_win_v, w_in, mu_shift, w0, w_decay_up, a0, w_aaa_up, k_k, k_a, r_k, gn_w, gn_b, sinks, w_branch_rwkv, w_branch_attn, w_out, ln_g, ln_b):
    assert w_in.shape[0] == DEPTH
    Bp = x_prompt.shape[0]
    Bs = x_sample.shape[0]
    assert Bp % QUAD == 0 and Bs % QUAD == 0
    o_gr = SHIFT_W
    o_k = W_Q + ATT_W
    o_ga = o_k + 2 * KV_W
    wi = w_in[0].astype(BF16)
    w = {"all": wi,
         "zskv": jnp.concatenate([_shift_k_major(wi[:, :o_gr]), wi[:, o_k:o_ga]], axis=1),
         "g_r": _k_major(wi[:, o_gr:W_Q], 1),
         "branch_r": _k_major(w_branch_rwkv[0], 0).astype(BF16),
         "branch_a": w_branch_attn[0].astype(BF16), "out": w_out[0].astype(BF16)}
    lane = jnp.arange(LANES)
    km = lambda a: _k_major(a.reshape(1, RWKV_W), 1)
    p = {"mu": _shift_k_major(mu_shift), "w0": km(w0), "wup": _k_major(w_decay_up[0], 1).astype(BF16),
         "a0": km(a0), "aup": _k_major(w_aaa_up[0], 1).astype(BF16), "k_k": km(k_k), "k_a": km(k_a),
         "r_k": km(r_k), "gn_w": km(gn_w), "gn_b": km(gn_b), "sinks": sinks[0], "ln_g": ln_g, "ln_b": ln_b,
         "jmat": (lane[:, None] % RWKV_HEADS == lane[None, :] % RWKV_HEADS).astype(BF16)}

    S0p = jnp.zeros((Bp, RWKV_HEADS, HEAD, HEAD), F32)
    sh0p = jnp.zeros((Bp, SHIFT_W), F32)
    yp, a1, a2, a3, a4 = _layer(x_prompt, S0p, sh0p, None, w, p)
    caches = (cache_win_k[0].reshape(Bs, WINDOW, KV_W), cache_win_v[0].reshape(Bs, WINDOW, KV_W))
    ys, b1, b2, b3, b4 = _layer(x_sample, state_wkv[0], state_shift[0], caches, w, p)
    return (yp, ys, a1[None], a2[None], a3[None], a4[None], b1[None], b2[None], b3[None], b4[None])
```

```python
import functools
import math

import jax
import jax.numpy as jnp
from jax import lax
from jax.experimental import pallas as pl
from jax.experimental.pallas import tpu as pltpu

RWKV_HEADS = 32
HEAD = 64
RWKV_W = RWKV_HEADS * HEAD
LORA = 128
SHIFT_W = 3 * RWKV_W + 2 * LORA
GN_EPS = 64e-5
N_Q_HEADS = 32
N_KV_HEADS = 4
ATT_W = N_Q_HEADS * HEAD
KV_W = N_KV_HEADS * HEAD
WINDOW = 128
ROT_DIM = HEAD // 4
ROPE_THETA = 500000.0
PAST_LEN = 8192
DEPTH = 1
ALPHA = (2.0 * DEPTH) ** 0.25
LN_EPS = 1e-5
LANES = 128
QUAD = LANES // RWKV_HEADS
DECAY_SCALE = math.exp(-0.5)
LOG2E = math.log2(math.e)
QK_SCALE = LOG2E / math.sqrt(HEAD)
ZSKV_W = SHIFT_W + 2 * KV_W
W_Q = SHIFT_W + RWKV_W
F_Q = 0
F_GA = ATT_W + 2 * KV_W
F_M = F_GA + ATT_W
F_GA_BLOCK = F_GA
W_TILE = 256
VMEM_LIMIT = 56 * 1024 * 1024

BF16 = jnp.bfloat16
F32 = jnp.float32


def _params(*sem):
    return pltpu.CompilerParams(dimension_semantics=sem, vmem_limit_bytes=VMEM_LIMIT)


def _tile(n, prefs):
    for p in prefs:
        if n % p == 0:
            return p
    return n


def _sigmoid(x):
    return 1.0 / (1.0 + jnp.exp(-x))


def _silu(x):
    return x * _sigmoid(x)


def _mm_kernel(x_ref, w_ref, o_ref):
    o_ref[...] = jnp.dot(x_ref[...], w_ref[...], preferred_element_type=F32).astype(o_ref.dtype)


def _matmul(x, w, out_dtype, name="proj"):
    M, K = x.shape
    N = w.shape[1]
    tm = _tile(M, (1024, 512, 256, 128))
    tn = _tile(N, (512, 768, 640, 256, 128))
    return pl.pallas_call(
        _mm_kernel,
        out_shape=jax.ShapeDtypeStruct((M, N), out_dtype),
        grid=(M // tm, N // tn),
        in_specs=[pl.BlockSpec((tm, K), lambda i, j: (i, 0)),
                  pl.BlockSpec((K, tn), lambda i, j: (0, j))],
        out_specs=pl.BlockSpec((tm, tn), lambda i, j: (i, j)),
        compiler_params=_params("parallel", "arbitrary"),
        name=name,
    )(x, w)


def _mm_blocks_kernel(x_ref, *refs):
    w_refs, o_ref = refs[:-1], refs[-1]
    x = x_ref[...]
    for c, w_ref in enumerate(w_refs):
        o_ref[:, c * W_TILE:(c + 1) * W_TILE] = jnp.dot(
            x, w_ref[...], preferred_element_type=F32).astype(o_ref.dtype)


def _matmul_cols(x, w, col0, out_dtype, name="proj"):
    M, K = x.shape
    N = w.shape[1] - col0
    assert col0 % W_TILE == 0 and N % W_TILE == 0
    nblk = _tile(N // W_TILE, (5, 4, 2))
    tm = _tile(M, (1024, 512, 256, 128))
    j0 = col0 // W_TILE
    wspec = lambda c: pl.BlockSpec((K, W_TILE), lambda i, j: (0, j0 + nblk * j + c))
    return pl.pallas_call(
        _mm_blocks_kernel,
        out_shape=jax.ShapeDtypeStruct((M, N), out_dtype),
        grid=(M // tm, N // (nblk * W_TILE)),
        in_specs=[pl.BlockSpec((tm, K), lambda i, j: (i, 0))] + [wspec(c) for c in range(nblk)],
        out_specs=pl.BlockSpec((tm, nblk * W_TILE), lambda i, j: (i, j)),
        compiler_params=_params("parallel", "arbitrary"),
        name=name,
    )(x, *([w] * nblk))


def _head_sum(x, j_ref):
    nt = x.shape[1] // LANES
    t = x[:, 0:LANES]
    for m in range(1, nt):
        t = t + x[:, m * LANES:(m + 1) * LANES]
    hi = t.astype(BF16)
    lo = (t - hi.astype(F32)).astype(BF16)
    s = (jnp.dot(hi, j_ref[...], preferred_element_type=F32)
         + jnp.dot(lo, j_ref[...], preferred_element_type=F32))
    return jnp.concatenate([s] * nt, axis=1)


def _lane_quarter_select(parts):
    lane = lax.broadcasted_iota(jnp.int32, parts[0].shape, 1)
    out = parts[QUAD - 1]
    for q in range(QUAD - 2, -1, -1):
        out = jnp.where(lane < (q + 1) * RWKV_HEADS, parts[q], out)
    return out


def _roll_lanes(x, shift):
    shift = shift % LANES
    return x if shift == 0 else pltpu.roll(x, shift, axis=1)


def _quad_transpose(tiles):
    rolled = []
    for s in range(QUAD):
        gathered = _lane_quarter_select([tiles[(g + s) % QUAD] for g in range(QUAD)])
        rolled.append(_roll_lanes(gathered, s * RWKV_HEADS))
    return [_lane_quarter_select([rolled[(a - c) % QUAD] for a in range(QUAD)]) for c in range(QUAD)]


def _interleave_store(xs, o_ref):
    for m in range(RWKV_W // LANES):
        outs = _quad_transpose([x[:, m * LANES:(m + 1) * LANES] for x in xs])
        for j in range(QUAD):
            k = m * QUAD + j
            o_ref[0, :, k * LANES:(k + 1) * LANES] = outs[j]


def _prep_kernel(zs_ref, sh_ref, mu_ref, w0_ref, wup_ref, a0_ref, aup_ref, kk_ref, ka_ref, rk_ref, j_ref,
                 r_o, w_o, kap_o, b_o, kt_o, v_o, bon_o, carry):
    @pl.when(pl.program_id(1) == 0)
    def _():
        carry[...] = sh_ref[...]

    tb = zs_ref.shape[1]
    row = lax.broadcasted_iota(jnp.int32, (tb, 1), 0)
    outs = [[] for _ in range(6)]
    for bq in range(QUAD):
        zs = zs_ref[bq]
        rolled = pltpu.roll(zs, 1, axis=0)
        prev = jnp.where(row == 0, carry[bq], rolled)
        carry[bq] = zs[tb - 1:tb, :]
        xs = zs + mu_ref[...] * (prev - zs)
        r = xs[:, 0:RWKV_W]
        k = xs[:, RWKV_W:2 * RWKV_W]
        v = xs[:, 2 * RWKV_W:3 * RWKV_W]
        wd = xs[:, 3 * RWKV_W:3 * RWKV_W + LORA]
        ad = xs[:, 3 * RWKV_W + LORA:SHIFT_W]
        wl = w0_ref[...] + jnp.dot(jnp.tanh(wd).astype(BF16), wup_ref[...], preferred_element_type=F32)
        dec = jnp.exp(-DECAY_SCALE * _sigmoid(wl))
        a = _sigmoid(a0_ref[...] + jnp.dot(ad.astype(BF16), aup_ref[...], preferred_element_type=F32))
        kkv = k * kk_ref[...]
        norm = jnp.maximum(jnp.sqrt(_head_sum(kkv * kkv, j_ref)), 1e-12)
        kap = kkv / norm
        kt = k * (1.0 + (a - 1.0) * ka_ref[...])
        bon_o[bq] = _head_sum(r * kt * rk_ref[...], j_ref) * v
        for lst, val in zip(outs, (r, dec, kap, kap * a, kt, v)):
            lst.append(val)
    for lst, o_ref in zip(outs, (r_o, w_o, kap_o, b_o, kt_o, v_o)):
        _interleave_store(lst, o_ref)


def _rwkv_prep(zs, shift0, p):
    B, T, _ = zs.shape
    G = B // QUAD
    tb = _tile(T, (32, 16, 8))
    row = lambda n: pl.BlockSpec((1, n), lambda g, i: (0, 0))
    full = lambda a: pl.BlockSpec(a.shape, lambda g, i: (0, 0))
    op = jax.ShapeDtypeStruct((G, T, HEAD * LANES), F32)
    opspec = pl.BlockSpec((1, tb, HEAD * LANES), lambda g, i: (g, i, 0))
    return pl.pallas_call(
        _prep_kernel,
        out_shape=(op,) * 6 + (jax.ShapeDtypeStruct((B, T, RWKV_W), F32),),
        grid=(G, T // tb),
        in_specs=[pl.BlockSpec((QUAD, tb, SHIFT_W), lambda g, i: (g, i, 0)),
                  pl.BlockSpec((QUAD, 1, SHIFT_W), lambda g, i: (g, 0, 0)),
                  row(SHIFT_W), row(RWKV_W), full(p["wup"]), row(RWKV_W), full(p["aup"]),
                  row(RWKV_W), row(RWKV_W), row(RWKV_W), full(p["jmat"])],
        out_specs=(opspec,) * 6 + (pl.BlockSpec((QUAD, tb, RWKV_W), lambda g, i: (g, i, 0)),),
        scratch_shapes=[pltpu.VMEM((QUAD, 1, SHIFT_W), F32)],
        compiler_params=_params("parallel", "arbitrary"),
        name="rwkv_prep",
    )(zs, shift0, p["mu"], p["w0"], p["wup"], p["a0"], p["aup"], p["k_k"], p["k_a"], p["r_k"], p["jmat"])


def _scan_kernel(r_ref, w_ref, kap_ref, b_ref, kt_ref, v_ref, s0_ref, y_ref, sout_ref, S):
    ti = pl.program_id(1)

    @pl.when(ti == 0)
    def _():
        S[...] = s0_ref[...]

    tb = r_ref.shape[1]

    u0 = S[0] * kap_ref[0, 0, 0:1, :]
    for k in range(1, HEAD):
        u0 = u0 + S[k] * kap_ref[0, 0, k:k + 1, :]

    def group_norm(y):
        m = jnp.mean(y, axis=0, keepdims=True)
        d = y - m
        var = jnp.mean(d * d, axis=0, keepdims=True)
        return d * lax.rsqrt(var + GN_EPS)

    def step(t, carry):
        u, y_prev = carry
        y_ref[0, jnp.maximum(t - 1, 0)] = group_norm(y_prev)
        tn = jnp.minimum(t + 1, tb - 1)
        vt = v_ref[0, t]
        y = None
        un = None
        for k in range(HEAD):
            s = S[k] * w_ref[0, t, k:k + 1, :] - u * b_ref[0, t, k:k + 1, :] + vt * kt_ref[0, t, k:k + 1, :]
            S[k] = s
            ty = s * r_ref[0, t, k:k + 1, :]
            tu = s * kap_ref[0, tn, k:k + 1, :]
            y = ty if y is None else y + ty
            un = tu if un is None else un + tu
        return un, y

    _, y_last = lax.fori_loop(0, tb, step, (u0, jnp.zeros((HEAD, LANES), F32)))
    y_ref[0, tb - 1] = group_norm(y_last)

    @pl.when(ti == pl.num_programs(1) - 1)
    def _():
        sout_ref[...] = S[...]


def _wkv_scan(ops, s0):
    G, T = ops[0].shape[:2]
    tb = _tile(T, (32, 16, 8))
    ospec = pl.BlockSpec((1, tb, HEAD, LANES), lambda g, i: (g, i, 0, 0))
    sspec = pl.BlockSpec((HEAD, HEAD, LANES), lambda g, i: (0, 0, g))
    return pl.pallas_call(
        _scan_kernel,
        out_shape=(jax.ShapeDtypeStruct((G, T, HEAD, LANES), F32),
                   jax.ShapeDtypeStruct((HEAD, HEAD, G * LANES), F32)),
        grid=(G, T // tb),
        in_specs=[ospec] * 6 + [sspec],
        out_specs=(ospec, sspec),
        scratch_shapes=[pltpu.VMEM((HEAD, HEAD, LANES), F32)],
        compiler_params=_params("parallel", "arbitrary"),
        name="wkv_scan",
    )(*ops, s0)


def _gate_kernel(yn_ref, bon_ref, g_ref, gw_ref, gb_ref, o_ref):
    for m in range(RWKV_W // LANES):
        yns = _quad_transpose([yn_ref[0, :, m * QUAD + j, :] for j in range(QUAD)])
        sl = slice(m * LANES, (m + 1) * LANES)
        for bq in range(QUAD):
            y = yns[bq] * gw_ref[:, sl] + gb_ref[:, sl] + bon_ref[bq, :, sl]
            o_ref[bq, :, sl] = (y * _silu(g_ref[bq, :, sl].astype(F32))).astype(o_ref.dtype)


def _rwkv_gate(yn, bonus, g, gn_w, gn_b):
    B, T, _ = bonus.shape
    G = B // QUAD
    tb = _tile(T, (128, 64, 32, 16, 8))
    spec = pl.BlockSpec((QUAD, tb, RWKV_W), lambda g_, i: (g_, i, 0))
    row = pl.BlockSpec((1, RWKV_W), lambda g_, i: (0, 0))
    return pl.pallas_call(
        _gate_kernel,
        out_shape=jax.ShapeDtypeStruct((B, T, RWKV_W), BF16),
        grid=(G, T // tb),
        in_specs=[pl.BlockSpec((1, tb, HEAD, LANES), lambda g_, i: (g_, i, 0, 0)), spec, spec, row, row],
        out_specs=spec,
        compiler_params=_params("parallel", "parallel"),
        name="rwkv_gate",
    )(yn, bonus, g, gn_w, gn_b)


def _rope_tables(pos):
    half = ROT_DIM // 2
    inv = ROPE_THETA ** (-jnp.arange(half, dtype=F32) * 2.0 / ROT_DIM)
    ang = pos.astype(F32)[:, None] * inv[None, :]
    cos, sin = jnp.cos(ang), jnp.sin(ang)
    n = pos.shape[0]
    one = jnp.ones((n, HEAD - ROT_DIM), F32)
    zero = jnp.zeros((n, HEAD - ROT_DIM), F32)
    zh = jnp.zeros((n, half), F32)
    c = jnp.concatenate([cos, cos, one], axis=1)
    s1 = jnp.concatenate([zh, sin, zero], axis=1)
    s2 = jnp.concatenate([-sin, zh, zero], axis=1)
    reps = LANES // HEAD
    return jnp.tile(c, (1, reps)), jnp.tile(s1, (1, reps)), jnp.tile(s2, (1, reps))


def _rope(x, c, s1, s2):
    half = ROT_DIM // 2
    return x * c + pltpu.roll(x, half, axis=1) * s1 + pltpu.roll(x, LANES - half, axis=1) * s2


def _dup_heads(x):
    sw = pltpu.roll(x, HEAD, axis=1)
    low = lax.broadcasted_iota(jnp.int32, x.shape, 1) < HEAD
    return jnp.where(low, x, sw), jnp.where(low, sw, x)


def _softmax_with_sink(s, sink):
    m = jnp.maximum(jnp.max(s, axis=1, keepdims=True), sink)
    e = jnp.exp2(s - m)
    den = jnp.sum(e, axis=1, keepdims=True) + jnp.exp2(sink - m)
    return e * (1.0 / den)


def _pair_attention(q, k2, v2, mask, sink0, sink1):
    n = q.shape[0]
    low = lax.broadcasted_iota(jnp.int32, q.shape, 1) < HEAD
    q = q * QK_SCALE
    qs = jnp.concatenate([jnp.where(low, q, 0.0), jnp.where(low, 0.0, q)], axis=0).astype(BF16)
    s = lax.dot_general(qs, k2, (((1,), (1,)), ((), ())), preferred_element_type=F32)
    s = jnp.where(mask, s, -1e30)
    top = lax.broadcasted_iota(jnp.int32, (2 * n, 1), 0) < n
    sink = jnp.where(top, sink0, sink1) * LOG2E
    p = _softmax_with_sink(s, sink).astype(BF16)
    o = jnp.dot(p, v2, preferred_element_type=F32)
    return jnp.where(low, o[:n], o[n:])


def _attn_prompt_kernel(sink_ref, q_ref, kc_ref, kp_ref, vc_ref, vp_ref, g_ref,
                        cc_ref, s1c_ref, s2c_ref, cp_ref, s1p_ref, s2p_ref, o_ref, krot_ref):
    i = pl.program_id(1)
    W = WINDOW
    qi = lax.broadcasted_iota(jnp.int32, (2 * W, 2 * W), 0) % W
    kj = lax.broadcasted_iota(jnp.int32, (2 * W, 2 * W), 1)
    rel = W + qi - kj
    mask = (rel >= 0) & (rel < W) & ((i > 0) | (kj >= W))
    cc, s1c, s2c = cc_ref[...], s1c_ref[...], s2c_ref[...]
    cp, s1p, s2p = cp_ref[...], s1p_ref[...], s2p_ref[...]
    k2, v2 = [], []
    for t in range(KV_W // LANES):
        sl = slice(t * LANES, (t + 1) * LANES)
        kc = _rope(kc_ref[:, sl], cc, s1c, s2c)
        krot_ref[:, sl] = kc
        kcat = jnp.concatenate([_rope(kp_ref[:, sl], cp, s1p, s2p), kc], axis=0)
        vcat = jnp.concatenate([vp_ref[:, sl], vc_ref[:, sl]], axis=0)
        k2.extend(x.astype(BF16) for x in _dup_heads(kcat))
        v2.extend(x.astype(BF16) for x in _dup_heads(vcat))
    pairs_per_kv = N_Q_HEADS // N_KV_HEADS // 2
    for p in range(N_Q_HEADS // 2):
        sl = slice(p * LANES, (p + 1) * LANES)
        q = _rope(q_ref[:, sl].astype(F32), cc, s1c, s2c)
        g = p // pairs_per_kv
        o = _pair_attention(q, k2[g], v2[g], mask, sink_ref[2 * p], sink_ref[2 * p + 1])
        o_ref[:, sl] = (o * _silu(g_ref[:, sl].astype(F32))).astype(o_ref.dtype)


def _attn_prompt(f, zskv, sinks, B, T):
    W = WINDOW
    nb = T // W
    tabs = _rope_tables(jnp.arange(T))
    kcol = SHIFT_W // KV_W
    cur = lambda b, i: (b * nb + i, 0)
    tcur = pl.BlockSpec((W, LANES), lambda b, i: (i, 0))
    tprv = pl.BlockSpec((W, LANES), lambda b, i: (jnp.maximum(i - 1, 0), 0))
    return pl.pallas_call(
        _attn_prompt_kernel,
        out_shape=(jax.ShapeDtypeStruct((B * T, ATT_W), BF16), jax.ShapeDtypeStruct((B * T, KV_W), F32)),
        grid=(B, nb),
        in_specs=[pl.BlockSpec(memory_space=pltpu.SMEM),
                  pl.BlockSpec((W, ATT_W), lambda b, i: (b * nb + i, F_Q // ATT_W)),
                  pl.BlockSpec((W, KV_W), lambda b, i: (b * nb + i, kcol)),
                  pl.BlockSpec((W, KV_W), lambda b, i: (b * nb + jnp.maximum(i - 1, 0), kcol)),
                  pl.BlockSpec((W, KV_W), lambda b, i: (b * nb + i, kcol + 1)),
                  pl.BlockSpec((W, KV_W), lambda b, i: (b * nb + jnp.maximum(i - 1, 0), kcol + 1)),
                  pl.BlockSpec((W, F_GA_BLOCK), lambda b, i: (b * nb + i, F_GA // F_GA_BLOCK)),
                  tcur, tcur, tcur, tprv, tprv, tprv],
        out_specs=(pl.BlockSpec((W, ATT_W), cur), pl.BlockSpec((W, KV_W), cur)),
        compiler_params=_params("parallel", "parallel"),
        name="attn_prompt",
    )(sinks, f, zskv, zskv, zskv, zskv, f, *tabs, *tabs)


def _attn_sample_kernel(q_ref, kn_ref, vn_ref, ck_ref, cv_ref, g_ref, sink_ref, c_ref, s1_ref, s2_ref,
                        o_ref, ko_ref, vo_ref):
    W = WINDOW
    nbat = ck_ref.shape[0]
    Ts = q_ref.shape[0] // nbat
    GQ = N_Q_HEADS // N_KV_HEADS
    GW = GQ * HEAD
    nrow = N_Q_HEADS * Ts
    tq = lax.broadcasted_iota(jnp.int32, (nrow, W + Ts), 0) % Ts
    kj = lax.broadcasted_iota(jnp.int32, (nrow, W + Ts), 1)
    rel = W + tq - kj
    mask = (rel >= 0) & (rel < W)
    sink = sink_ref[...] * LOG2E
    lane_head = lax.broadcasted_iota(jnp.int32, (Ts, GW), 1) // HEAD
    c, s1, s2 = c_ref[...], s1_ref[...], s2_ref[...]
    tiles_per_group = GW // LANES
    for bi in range(nbat):
        rows = slice(bi * Ts, (bi + 1) * Ts)
        k2, v2 = [], []
        for t in range(KV_W // LANES):
            sl = slice(t * LANES, (t + 1) * LANES)
            kcat = jnp.concatenate([ck_ref[bi, :, sl], _rope(kn_ref[rows, sl], c, s1, s2)], axis=0)
            vcat = jnp.concatenate([cv_ref[bi, :, sl], vn_ref[rows, sl]], axis=0)
            ko_ref[bi, :, sl] = kcat[Ts:, :]
            vo_ref[bi, :, sl] = vcat[Ts:, :]
            k2.extend(x.astype(BF16) for x in _dup_heads(kcat))
            v2.extend(x.astype(BF16) for x in _dup_heads(vcat))
        s_parts = []
        for g in range(N_KV_HEADS):
            qg = jnp.concatenate(
                [_rope(q_ref[rows, (g * tiles_per_group + i) * LANES:(g * tiles_per_group + i + 1) * LANES]
                       .astype(F32), c, s1, s2)
                 for i in range(tiles_per_group)], axis=1)
            qg = qg * QK_SCALE
            qs = jnp.concatenate([jnp.where(lane_head == h, qg, 0.0) for h in range(GQ)], axis=0).astype(BF16)
            k8 = jnp.concatenate([k2[g]] * tiles_per_group, axis=1)
            s_parts.append(lax.dot_general(qs, k8, (((1,), (1,)), ((), ())), preferred_element_type=F32))
        s = jnp.where(mask, jnp.concatenate(s_parts, axis=0), -1e30)
        p = _softmax_with_sink(s, sink).astype(BF16)
        for g in range(N_KV_HEADS):
            v8 = jnp.concatenate([v2[g]] * tiles_per_group, axis=1)
            o = jnp.dot(p[g * GQ * Ts:(g + 1) * GQ * Ts], v8, preferred_element_type=F32)
            og = o[0:Ts]
            for h in range(1, GQ):
                og = jnp.where(lane_head == h, o[h * Ts:(h + 1) * Ts], og)
            gl = slice(g * GW, (g + 1) * GW)
            o_ref[rows, gl] = (og * _silu(g_ref[rows, gl].astype(F32))).astype(o_ref.dtype)


def _attn_sample(f, zskv, sinks, cache_k, cache_v, B, Ts):
    W = WINDOW
    nbat = _tile(B, (4, 2, 1))
    tabs = _rope_tables(PAST_LEN + jnp.arange(Ts))
    kcol = SHIFT_W // KV_W
    rows = nbat * Ts
    sink_col = jnp.repeat(sinks, Ts).reshape(N_Q_HEADS * Ts, 1)
    cspec = pl.BlockSpec((nbat, W, KV_W), lambda i: (i, 0, 0))
    tspec = pl.BlockSpec((Ts, LANES), lambda i: (0, 0))
    cache = jax.ShapeDtypeStruct((B, W, KV_W), F32)
    return pl.pallas_call(
        _attn_sample_kernel,
        out_shape=(jax.ShapeDtypeStruct((B * Ts, ATT_W), BF16), cache, cache),
        grid=(B // nbat,),
        in_specs=[pl.BlockSpec((rows, ATT_W), lambda i: (i, F_Q // ATT_W)),
                  pl.BlockSpec((rows, KV_W), lambda i: (i, kcol)),
                  pl.BlockSpec((rows, KV_W), lambda i: (i, kcol + 1)),
                  cspec, cspec,
                  pl.BlockSpec((rows, F_GA_BLOCK), lambda i: (i, F_GA // F_GA_BLOCK)),
                  pl.BlockSpec((N_Q_HEADS * Ts, 1), lambda i: (0, 0)),
                  tspec, tspec, tspec],
        out_specs=(pl.BlockSpec((rows, ATT_W), lambda i: (i, 0)), cspec, cspec),
        compiler_params=_params("parallel"),
        name="attn_sample",
    )(f, zskv, zskv, cache_k, cache_v, f, sink_col, *tabs)


def _mix_kernel(yr_ref, ya_ref, wr_ref, wa_ref, gr_ref, ga_ref, o_ref):
    br = jnp.dot(yr_ref[...], wr_ref[...], preferred_element_type=F32)
    ba = jnp.dot(ya_ref[...], wa_ref[...], preferred_element_type=F32)
    o_ref[...] = (_sigmoid(gr_ref[...].astype(F32)) * br
                  + _sigmoid(ga_ref[...].astype(F32)) * ba).astype(o_ref.dtype)


def _mix(y_r, y_a, w_r, w_a, f):
    M = y_r.shape[0]
    D = w_r.shape[1]
    tm = _tile(M, (1024, 512, 256, 128))
    tn = _tile(D, (512, 256, 128))
    nj = D // tn
    j0 = F_M // tn
    return pl.pallas_call(
        _mix_kernel,
        out_shape=jax.ShapeDtypeStruct((M, D), BF16),
        grid=(M // tm, nj),
        in_specs=[pl.BlockSpec((tm, RWKV_W), lambda i, j: (i, 0)),
                  pl.BlockSpec((tm, ATT_W), lambda i, j: (i, 0)),
                  pl.BlockSpec((RWKV_W, tn), lambda i, j: (0, j)),
                  pl.BlockSpec((ATT_W, tn), lambda i, j: (0, j)),
                  pl.BlockSpec((tm, tn), lambda i, j: (i, j0 + j)),
                  pl.BlockSpec((tm, tn), lambda i, j: (i, j0 + nj + j))],
        out_specs=pl.BlockSpec((tm, tn), lambda i, j: (i, j)),
        compiler_params=_params("parallel", "arbitrary"),
        name="branch_mix",
    )(y_r, y_a, w_r, w_a, f, f)


def _out_kernel(mix_ref, w_ref, x_ref, g_ref, b_ref, o_ref, *, tn):
    j = pl.program_id(1)
    col = pl.multiple_of(j * tn, tn)
    out = jnp.dot(mix_ref[...], w_ref[...], preferred_element_type=F32)
    o_ref[:, pl.ds(col, tn)] = ALPHA * x_ref[...] + out

    @pl.when(j == pl.num_programs(1) - 1)
    def _():
        h = o_ref[...]
        m = jnp.mean(h, axis=1, keepdims=True)
        d = h - m
        var = jnp.mean(d * d, axis=1, keepdims=True)
        o_ref[...] = d * lax.rsqrt(var + LN_EPS) * g_ref[...] + b_ref[...]


def _out_ln(mix, w_out, x, ln_g, ln_b):
    M, D = x.shape
    tm = _tile(M, (512, 256, 128))
    tn = _tile(D, (512, 256, 128))
    row = pl.BlockSpec((1, D), lambda i, j: (0, 0))
    return pl.pallas_call(
        functools.partial(_out_kernel, tn=tn),
        out_shape=jax.ShapeDtypeStruct((M, D), F32),
        grid=(M // tm, D // tn),
        in_specs=[pl.BlockSpec((tm, D), lambda i, j: (i, 0)),
                  pl.BlockSpec((D, tn), lambda i, j: (0, j)),
                  pl.BlockSpec((tm, tn), lambda i, j: (i, j)),
                  row, row],
        out_specs=pl.BlockSpec((tm, D), lambda i, j: (i, 0)),
        compiler_params=_params("parallel", "arbitrary"),
        name="out_ln",
    )(mix, w_out, x, ln_g, ln_b)


def _k_major(x, axis):
    shp = x.shape
    n = shp[axis] // RWKV_W
    y = x.reshape(shp[:axis] + (n, RWKV_HEADS, HEAD) + shp[axis + 1:])
    y = jnp.swapaxes(y, axis + 1, axis + 2)
    return y.reshape(shp)


def _head_major(x, axis):
    shp = x.shape
    n = shp[axis] // RWKV_W
    y = x.reshape(shp[:axis] + (n, HEAD, RWKV_HEADS) + shp[axis + 1:])
    y = jnp.swapaxes(y, axis + 1, axis + 2)
    return y.reshape(shp)


def _shift_k_major(x):
    return jnp.concatenate([_k_major(x[..., :3 * RWKV_W], x.ndim - 1), x[..., 3 * RWKV_W:]], axis=-1)


def _shift_head_major(x):
    return jnp.concatenate([_head_major(x[..., :3 * RWKV_W], x.ndim - 1), x[..., 3 * RWKV_W:]], axis=-1)


def _layer(x, S0, shift0, caches, w, p):
    B, T, D = x.shape
    M = B * T
    G = B // QUAD
    x2 = x.reshape(M, D)
    xb = x2.astype(BF16)
    zskv = _matmul(xb, w["zskv"], F32, name="proj_shift_kv")
    zskv3 = zskv.reshape(B, T, ZSKV_W)
    prep = _rwkv_prep(zskv3, _shift_k_major(shift0).reshape(B, 1, SHIFT_W), p)
    ops = [a.reshape(G, T, HEAD, LANES) for a in prep[:6]]
    s0 = S0.transpose(3, 2, 0, 1).reshape(HEAD, HEAD, B * RWKV_HEADS)
    f = _matmul_cols(xb, w["all"], W_Q, BF16, name="proj_q_gates")
    g_r = _matmul(xb, w["g_r"], BF16, name="proj_gate_r")
    yn, s_fin = _wkv_scan(ops, s0)
    y_r = _rwkv_gate(yn, prep[6], g_r.reshape(B, T, RWKV_W), p["gn_w"], p["gn_b"]).reshape(M, RWKV_W)
    S_new = s_fin.reshape(HEAD, HEAD, B, RWKV_HEADS).transpose(2, 3, 1, 0)
    shift_new = _shift_head_major(zskv3[:, T - 1, :SHIFT_W])

    if caches is None:
        y_a, k_rot = _attn_prompt(f, zskv, p["sinks"], B, T)
        k_buf = k_rot.reshape(B, T, N_KV_HEADS, HEAD)[:, T - WINDOW:]
        v_buf = zskv3[:, T - WINDOW:, SHIFT_W + KV_W:].reshape(B, WINDOW, N_KV_HEADS, HEAD)
    else:
        y_a, k_buf, v_buf = _attn_sample(f, zskv, p["sinks"], caches[0], caches[1], B, T)
        k_buf = k_buf.reshape(B, WINDOW, N_KV_HEADS, HEAD)
        v_buf = v_buf.reshape(B, WINDOW, N_KV_HEADS, HEAD)

    mix = _mix(y_r, y_a, w["branch_r"], w["branch_a"], f)
    y = _out_ln(mix, w["out"], x2, p["ln_g"], p["ln_b"]).reshape(B, T, D)
    return y, S_new, shift_new, k_buf, v_buf


def kernel(x_prompt, x_sample, state_wkv, state_shift, cache_win_k, cache_win_v, w_in, mu_shift, w0, w_decay_up, a0, w_aaa_up, k_k, k_a, r_k, gn_w, gn_b, sinks, w_branch_rwkv, w_branch_attn, w_out, ln_g, ln_b):
    assert w_in.shape[0] == DEPTH
    Bp = x_prompt.shape[0]
    Bs = x_sample.shape[0]
    assert Bp % QUAD == 0 and Bs % QUAD == 0
    o_gr = SHIFT_W
    o_k = W_Q + ATT_W
    o_ga = o_k + 2 * KV_W
    wi = w_in[0].astype(BF16)
    w = {"all": wi,
         "zskv": jnp.concatenate([_shift_k_major(wi[:, :o_gr]), wi[:, o_k:o_ga]], axis=1),
         "g_r": _k_major(wi[:, o_gr:W_Q], 1),
         "branch_r": _k_major(w_branch_rwkv[0], 0).astype(BF16),
         "branch_a": w_branch_attn[0].astype(BF16), "out": w_out[0].astype(BF16)}
    lane = jnp.arange(LANES)
    km = lambda a: _k_major(a.reshape(1, RWKV_W), 1)
    p = {"mu": _shift_k_major(mu_shift), "w0": km(w0), "wup": _k_major(w_decay_up[0], 1).astype(BF16),
         "a0": km(a0), "aup": _k_major(w_aaa_up[0], 1).astype(BF16), "k_k": km(k_k), "k_a": km(k_a),
         "r_k": km(r_k), "gn_w": km(gn_w), "gn_b": km(gn_b), "sinks": sinks[0], "ln_g": ln_g, "ln_b": ln_b,
         "jmat": (lane[:, None] % RWKV_HEADS == lane[None, :] % RWKV_HEADS).astype(BF16)}

    S0p = jnp.zeros((Bp, RWKV_HEADS, HEAD, HEAD), F32)
    sh0p = jnp.zeros((Bp, SHIFT_W), F32)
    yp, a1, a2, a3, a4 = _layer(x_prompt, S0p, sh0p, None, w, p)
    caches = (cache_win_k[0].reshape(Bs, WINDOW, KV_W), cache_win_v[0].reshape(Bs, WINDOW, KV_W))
    ys, b1, b2, b3, b4 = _layer(x_sample, state_wkv[0], state_shift[0], caches, w, p)
    return (yp, ys, a1[None], a2[None], a3[None], a4[None], b1[None], b2[None], b3[None], b4[None])
```

```python
import functools
import math

import jax
import jax.numpy as jnp
from jax import lax
from jax.experimental import pallas as pl
from jax.experimental.pallas import tpu as pltpu

RWKV_HEADS = 32
HEAD = 64
RWKV_W = RWKV_HEADS * HEAD
LORA = 128
SHIFT_W = 3 * RWKV_W + 2 * LORA
GN_EPS = 64e-5
N_Q_HEADS = 32
N_KV_HEADS = 4
ATT_W = N_Q_HEADS * HEAD
KV_W = N_KV_HEADS * HEAD
WINDOW = 128
ROT_DIM = HEAD // 4
ROPE_THETA = 500000.0
PAST_LEN = 8192
DEPTH = 1
ALPHA = (2.0 * DEPTH) ** 0.25
LN_EPS = 1e-5
LANES = 128
QUAD = LANES // RWKV_HEADS
DECAY_SCALE = math.exp(-0.5)
LOG2E = math.log2(math.e)
QK_SCALE = LOG2E / math.sqrt(HEAD)
ZSKV_W = SHIFT_W + 2 * KV_W
W_Q = SHIFT_W + RWKV_W
F_Q = 0
F_GA = ATT_W + 2 * KV_W
F_M = F_GA + ATT_W
F_GA_BLOCK = F_GA
W_TILE = 256
VMEM_LIMIT = 56 * 1024 * 1024

BF16 = jnp.bfloat16
F32 = jnp.float32


def _params(*sem):
    return pltpu.CompilerParams(dimension_semantics=sem, vmem_limit_bytes=VMEM_LIMIT)


def _tile(n, prefs):
    for p in prefs:
        if n % p == 0:
            return p
    return n


def _sigmoid(x):
    return 1.0 / (1.0 + jnp.exp(-x))


def _silu(x):
    return x * _sigmoid(x)


def _mm_kernel(x_ref, w_ref, o_ref):
    o_ref[...] = jnp.dot(x_ref[...], w_ref[...], preferred_element_type=F32).astype(o_ref.dtype)


def _matmul(x, w, out_dtype, name="proj"):
    M, K = x.shape
    N = w.shape[1]
    tm = _tile(M, (1024, 512, 256, 128))
    tn = _tile(N, (512, 768, 640, 256, 128))
    return pl.pallas_call(
        _mm_kernel,
        out_shape=jax.ShapeDtypeStruct((M, N), out_dtype),
        grid=(M // tm, N // tn),
        in_specs=[pl.BlockSpec((tm, K), lambda i, j: (i, 0)),
                  pl.BlockSpec((K, tn), lambda i, j: (0, j))],
        out_specs=pl.BlockSpec((tm, tn), lambda i, j: (i, j)),
        compiler_params=_params("parallel", "arbitrary"),
        name=name,
    )(x, w)


def _mm_blocks_kernel(x_ref, *refs):
    w_refs, o_ref = refs[:-1], refs[-1]
    x = x_ref[...]
    for c, w_ref in enumerate(w_refs):
        o_ref[:, c * W_TILE:(c + 1) * W_TILE] = jnp.dot(
            x, w_ref[...], preferred_element_type=F32).astype(o_ref.dtype)


def _matmul_cols(x, w, col0, out_dtype, name="proj"):
    M, K = x.shape
    N = w.shape[1] - col0
    assert col0 % W_TILE == 0 and N % W_TILE == 0
    nblk = _tile(N // W_TILE, (5, 4, 2))
    tm = _tile(M, (1024, 512, 256, 128))
    j0 = col0 // W_TILE
    wspec = lambda c: pl.BlockSpec((K, W_TILE), lambda i, j: (0, j0 + nblk * j + c))
    return pl.pallas_call(
        _mm_blocks_kernel,
        out_shape=jax.ShapeDtypeStruct((M, N), out_dtype),
        grid=(M // tm, N // (nblk * W_TILE)),
        in_specs=[pl.BlockSpec((tm, K), lambda i, j: (i, 0))] + [wspec(c) for c in range(nblk)],
        out_specs=pl.BlockSpec((tm, nblk * W_TILE), lambda i, j: (i, j)),
        compiler_params=_params("parallel", "arbitrary"),
        name=name,
    )(x, *([w] * nblk))


def _head_sum(x, j_ref):
    nt = x.shape[1] // LANES
    t = x[:, 0:LANES]
    for m in range(1, nt):
        t = t + x[:, m * LANES:(m + 1) * LANES]
    hi = t.astype(BF16)
    lo = (t - hi.astype(F32)).astype(BF16)
    s = (jnp.dot(hi, j_ref[...], preferred_element_type=F32)
         + jnp.dot(lo, j_ref[...], preferred_element_type=F32))
    return jnp.concatenate([s] * nt, axis=1)


def _lane_quarter_select(parts):
    lane = lax.broadcasted_iota(jnp.int32, parts[0].shape, 1)
    out = parts[QUAD - 1]
    for q in range(QUAD - 2, -1, -1):
        out = jnp.where(lane < (q + 1) * RWKV_HEADS, parts[q], out)
    return out


def _roll_lanes(x, shift):
    shift = shift % LANES
    return x if shift == 0 else pltpu.roll(x, shift, axis=1)


def _quad_transpose(tiles):
    rolled = []
    for s in range(QUAD):
        gathered = _lane_quarter_select([tiles[(g + s) % QUAD] for g in range(QUAD)])
        rolled.append(_roll_lanes(gathered, s * RWKV_HEADS))
    return [_lane_quarter_select([rolled[(a - c) % QUAD] for a in range(QUAD)]) for c in range(QUAD)]


def _interleave_store(xs, o_ref):
    for m in range(RWKV_W // LANES):
        outs = _quad_transpose([x[:, m * LANES:(m + 1) * LANES] for x in xs])
        for j in range(QUAD):
            k = m * QUAD + j
            o_ref[0, :, k * LANES:(k + 1) * LANES] = outs[j]


def _prep_kernel(zs_ref, sh_ref, mu_ref, w0_ref, wup_ref, a0_ref, aup_ref, kk_ref, ka_ref, rk_ref, j_ref,
                 r_o, w_o, kap_o, b_o, kt_o, v_o, bon_o, carry):
    @pl.when(pl.program_id(1) == 0)
    def _():
        carry[...] = sh_ref[...]

    tb = zs_ref.shape[1]
    row = lax.broadcasted_iota(jnp.int32, (tb, 1), 0)
    outs = [[] for _ in range(6)]
    for bq in range(QUAD):
        zs = zs_ref[bq]
        rolled = pltpu.roll(zs, 1, axis=0)
        prev = jnp.where(row == 0, carry[bq], rolled)
        carry[bq] = zs[tb - 1:tb, :]
        xs = zs + mu_ref[...] * (prev - zs)
        r = xs[:, 0:RWKV_W]
        k = xs[:, RWKV_W:2 * RWKV_W]
        v = xs[:, 2 * RWKV_W:3 * RWKV_W]
        wd = xs[:, 3 * RWKV_W:3 * RWKV_W + LORA]
        ad = xs[:, 3 * RWKV_W + LORA:SHIFT_W]
        wl = w0_ref[...] + jnp.dot(jnp.tanh(wd).astype(BF16), wup_ref[...], preferred_element_type=F32)
        dec = jnp.exp(-DECAY_SCALE * _sigmoid(wl))
        a = _sigmoid(a0_ref[...] + jnp.dot(ad.astype(BF16), aup_ref[...], preferred_element_type=F32))
        kkv = k * kk_ref[...]
        norm = jnp.maximum(jnp.sqrt(_head_sum(kkv * kkv, j_ref)), 1e-12)
        kap = kkv / norm
        kt = k * (1.0 + (a - 1.0) * ka_ref[...])
        bon_o[bq] = _head_sum(r * kt * rk_ref[...], j_ref) * v
        for lst, val in zip(outs, (r, dec, kap, kap * a, kt, v)):
            lst.append(val)
    for lst, o_ref in zip(outs, (r_o, w_o, kap_o, b_o, kt_o, v_o)):
        _interleave_store(lst, o_ref)


def _rwkv_prep(zs, shift0, p):
    B, T, _ = zs.shape
    G = B // QUAD
    tb = _tile(T, (32, 16, 8))
    row = lambda n: pl.BlockSpec((1, n), lambda g, i: (0, 0))
    full = lambda a: pl.BlockSpec(a.shape, lambda g, i: (0, 0))
    op = jax.ShapeDtypeStruct((G, T, HEAD * LANES), F32)
    opspec = pl.BlockSpec((1, tb, HEAD * LANES), lambda g, i: (g, i, 0))
    return pl.pallas_call(
        _prep_kernel,
        out_shape=(op,) * 6 + (jax.ShapeDtypeStruct((B, T, RWKV_W), F32),),
        grid=(G, T // tb),
        in_specs=[pl.BlockSpec((QUAD, tb, SHIFT_W), lambda g, i: (g, i, 0)),
                  pl.BlockSpec((QUAD, 1, SHIFT_W), lambda g, i: (g, 0, 0)),
                  row(SHIFT_W), row(RWKV_W), full(p["wup"]), row(RWKV_W), full(p["aup"]),
                  row(RWKV_W), row(RWKV_W), row(RWKV_W), full(p["jmat"])],
        out_specs=(opspec,) * 6 + (pl.BlockSpec((QUAD, tb, RWKV_W), lambda g, i: (g, i, 0)),),
        scratch_shapes=[pltpu.VMEM((QUAD, 1, SHIFT_W), F32)],
        compiler_params=_params("parallel", "arbitrary"),
        name="rwkv_prep",
    )(zs, shift0, p["mu"], p["w0"], p["wup"], p["a0"], p["aup"], p["k_k"], p["k_a"], p["r_k"], p["jmat"])


def _scan_kernel(r_ref, w_ref, kap_ref, b_ref, kt_ref, v_ref, s0_ref, y_ref, sout_ref, S):
    ti = pl.program_id(1)

    @pl.when(ti == 0)
    def _():
        S[...] = s0_ref[...]

    tb = r_ref.shape[1]

    u0 = S[0] * kap_ref[0, 0, 0:1, :]
    for k in range(1, HEAD):
        u0 = u0 + S[k] * kap_ref[0, 0, k:k + 1, :]

    def group_norm(y):
        m = jnp.mean(y, axis=0, keepdims=True)
        d = y - m
        var = jnp.mean(d * d, axis=0, keepdims=True)
        return d * lax.rsqrt(var + GN_EPS)

    def step(t, carry):
        u, y_prev = carry
        y_ref[0, jnp.maximum(t - 1, 0)] = group_norm(y_prev)
        tn = jnp.minimum(t + 1, tb - 1)
        vt = v_ref[0, t]
        y = None
        un = None
        for k in range(HEAD):
            s = S[k] * w_ref[0, t, k:k + 1, :] - u * b_ref[0, t, k:k + 1, :] + vt * kt_ref[0, t, k:k + 1, :]
            S[k] = s
            ty = s * r_ref[0, t, k:k + 1, :]
            tu = s * kap_ref[0, tn, k:k + 1, :]
            y = ty if y is None else y + ty
            un = tu if un is None else un + tu
        return un, y

    _, y_last = lax.fori_loop(0, tb, step, (u0, jnp.zeros((HEAD, LANES), F32)))
    y_ref[0, tb - 1] = group_norm(y_last)

    @pl.when(ti == pl.num_programs(1) - 1)
    def _():
        sout_ref[...] = S[...]


def _wkv_scan(ops, s0):
    G, T = ops[0].shape[:2]
    tb = _tile(T, (32, 16, 8))
    ospec = pl.BlockSpec((1, tb, HEAD, LANES), lambda g, i: (g, i, 0, 0))
    sspec = pl.BlockSpec((HEAD, HEAD, LANES), lambda g, i: (0, 0, g))
    return pl.pallas_call(
        _scan_kernel,
        out_shape=(jax.ShapeDtypeStruct((G, T, HEAD, LANES), F32),
                   jax.ShapeDtypeStruct((HEAD, HEAD, G * LANES), F32)),
        grid=(G, T // tb),
        in_specs=[ospec] * 6 + [sspec],
        out_specs=(ospec, sspec),
        scratch_shapes=[pltpu.VMEM((HEAD, HEAD, LANES), F32)],
        compiler_params=_params("parallel", "arbitrary"),
        name="wkv_scan",
    )(*ops, s0)


def _gate_kernel(yn_ref, bon_ref, g_ref, gw_ref, gb_ref, o_ref):
    for m in range(RWKV_W // LANES):
        yns = _quad_transpose([yn_ref[0, :, m * QUAD + j, :] for j in range(QUAD)])
        sl = slice(m * LANES, (m + 1) * LANES)
        for bq in range(QUAD):
            y = yns[bq] * gw_ref[:, sl] + gb_ref[:, sl] + bon_ref[bq, :, sl]
            o_ref[bq, :, sl] = (y * _silu(g_ref[bq, :, sl].astype(F32))).astype(o_ref.dtype)


def _rwkv_gate(yn, bonus, g, gn_w, gn_b):
    B, T, _ = bonus.shape
    G = B // QUAD
    tb = _tile(T, (128, 64, 32, 16, 8))
    spec = pl.BlockSpec((QUAD, tb, RWKV_W), lambda g_, i: (g_, i, 0))
    row = pl.BlockSpec((1, RWKV_W), lambda g_, i: (0, 0))
    return pl.pallas_call(
        _gate_kernel,
        out_shape=jax.ShapeDtypeStruct((B, T, RWKV_W), BF16),
        grid=(G, T // tb),
        in_specs=[pl.BlockSpec((1, tb, HEAD, LANES), lambda g_, i: (g_, i, 0, 0)), spec, spec, row, row],
        out_specs=spec,
        compiler_params=_params("parallel", "parallel"),
        name="rwkv_gate",
    )(yn, bonus, g, gn_w, gn_b)


def _rope_tables(pos):
    half = ROT_DIM // 2
    inv = ROPE_THETA ** (-jnp.arange(half, dtype=F32) * 2.0 / ROT_DIM)
    ang = pos.astype(F32)[:, None] * inv[None, :]
    cos, sin = jnp.cos(ang), jnp.sin(ang)
    n = pos.shape[0]
    one = jnp.ones((n, HEAD - ROT_DIM), F32)
    zero = jnp.zeros((n, HEAD - ROT_DIM), F32)
    zh = jnp.zeros((n, half), F32)
    c = jnp.concatenate([cos, cos, one], axis=1)
    s1 = jnp.concatenate([zh, sin, zero], axis=1)
    s2 = jnp.concatenate([-sin, zh, zero], axis=1)
    reps = LANES // HEAD
    return jnp.tile(c, (1, reps)), jnp.tile(s1, (1, reps)), jnp.tile(s2, (1, reps))


def _rope(x, c, s1, s2):
    half = ROT_DIM // 2
    return x * c + pltpu.roll(x, half, axis=1) * s1 + pltpu.roll(x, LANES - half, axis=1) * s2


def _dup_heads(x):
    sw = pltpu.roll(x, HEAD, axis=1)
    low = lax.broadcasted_iota(jnp.int32, x.shape, 1) < HEAD
    return jnp.where(low, x, sw), jnp.where(low, sw, x)


def _softmax_with_sink(s, sink):
    m = jnp.maximum(jnp.max(s, axis=1, keepdims=True), sink)
    e = jnp.exp2(s - m)
    den = jnp.sum(e, axis=1, keepdims=True) + jnp.exp2(sink - m)
    return e * (1.0 / den)


def _pair_attention(q, k2, v2, mask, sink0, sink1):
    n = q.shape[0]
    low = lax.broadcasted_iota(jnp.int32, q.shape, 1) < HEAD
    q = q * QK_SCALE
    qs = jnp.concatenate([jnp.where(low, q, 0.0), jnp.where(low, 0.0, q)], axis=0).astype(BF16)
    s = lax.dot_general(qs, k2, (((1,), (1,)), ((), ())), preferred_element_type=F32)
    s = jnp.where(mask, s, -1e30)
    top = lax.broadcasted_iota(jnp.int32, (2 * n, 1), 0) < n
    sink = jnp.where(top, sink0, sink1) * LOG2E
    p = _softmax_with_sink(s, sink).astype(BF16)
    o = jnp.dot(p, v2, preferred_element_type=F32)
    return jnp.where(low, o[:n], o[n:])


def _attn_prompt_kernel(sink_ref, q_ref, kc_ref, kp_ref, vc_ref, vp_ref, g_ref,
                        cc_ref, s1c_ref, s2c_ref, cp_ref, s1p_ref, s2p_ref, o_ref, krot_ref):
    i = pl.program_id(1)
    W = WINDOW
    qi = lax.broadcasted_iota(jnp.int32, (2 * W, 2 * W), 0) % W
    kj = lax.broadcasted_iota(jnp.int32, (2 * W, 2 * W), 1)
    rel = W + qi - kj
    mask = (rel >= 0) & (rel < W) & ((i > 0) | (kj >= W))
    cc, s1c, s2c = cc_ref[...], s1c_ref[...], s2c_ref[...]
    cp, s1p, s2p = cp_ref[...], s1p_ref[...], s2p_ref[...]
    k2, v2 = [], []
    for t in range(KV_W // LANES):
        sl = slice(t * LANES, (t + 1) * LANES)
        kc = _rope(kc_ref[:, sl], cc, s1c, s2c)
        krot_ref[:, sl] = kc
        kcat = jnp.concatenate([_rope(kp_ref[:, sl], cp, s1p, s2p), kc], axis=0)
        vcat = jnp.concatenate([vp_ref[:, sl], vc_ref[:, sl]], axis=0)
        k2.extend(x.astype(BF16) for x in _dup_heads(kcat))
        v2.extend(x.astype(BF16) for x in _dup_heads(vcat))
    pairs_per_kv = N_Q_HEADS // N_KV_HEADS // 2
    for p in range(N_Q_HEADS // 2):
        sl = slice(p * LANES, (p + 1) * LANES)
        q = _rope(q_ref[:, sl].astype(F32), cc, s1c, s2c)
        g = p // pairs_per_kv
        o = _pair_attention(q, k2[g], v2[g], mask, sink_ref[2 * p], sink_ref[2 * p + 1])
        o_ref[:, sl] = (o * _silu(g_ref[:, sl].astype(F32))).astype(o_ref.dtype)


def _attn_prompt(f, zskv, sinks, B, T):
    W = WINDOW
    nb = T // W
    tabs = _rope_tables(jnp.arange(T))
    kcol = SHIFT_W // KV_W
    cur = lambda b, i: (b * nb + i, 0)
    tcur = pl.BlockSpec((W, LANES), lambda b, i: (i, 0))
    tprv = pl.BlockSpec((W, LANES), lambda b, i: (jnp.maximum(i - 1, 0), 0))
    return pl.pallas_call(
        _attn_prompt_kernel,
        out_shape=(jax.ShapeDtypeStruct((B * T, ATT_W), BF16), jax.ShapeDtypeStruct((B * T, KV_W), F32)),
        grid=(B, nb),
        in_specs=[pl.BlockSpec(memory_space=pltpu.SMEM),
                  pl.BlockSpec((W, ATT_W), lambda b, i: (b * nb + i, F_Q // ATT_W)),
                  pl.BlockSpec((W, KV_W), lambda b, i: (b * nb + i, kcol)),
                  pl.BlockSpec((W, KV_W), lambda b, i: (b * nb + jnp.maximum(i - 1, 0), kcol)),
                  pl.BlockSpec((W, KV_W), lambda b, i: (b * nb + i, kcol + 1)),
                  pl.BlockSpec((W, KV_W), lambda b, i: (b * nb + jnp.maximum(i - 1, 0), kcol + 1)),
                  pl.BlockSpec((W, F_GA_BLOCK), lambda b, i: (b * nb + i, F_GA // F_GA_BLOCK)),
                  tcur, tcur, tcur, tprv, tprv, tprv],
        out_specs=(pl.BlockSpec((W, ATT_W), cur), pl.BlockSpec((W, KV_W), cur)),
        compiler_params=_params("parallel", "parallel"),
        name="attn_prompt",
    )(sinks, f, zskv, zskv, zskv, zskv, f, *tabs, *tabs)


def _attn_sample_kernel(q_ref, kn_ref, vn_ref, ck_ref, cv_ref, g_ref, sink_ref, c_ref, s1_ref, s2_ref,
                        o_ref, ko_ref, vo_ref):
    W = WINDOW
    nbat = ck_ref.shape[0]
    Ts = q_ref.shape[0] // nbat
    GQ = N_Q_HEADS // N_KV_HEADS
    GW = GQ * HEAD
    nrow = N_Q_HEADS * Ts
    tq = lax.broadcasted_iota(jnp.int32, (nrow, W + Ts), 0) % Ts
    kj = lax.broadcasted_iota(jnp.int32, (nrow, W + Ts), 1)
    rel = W + tq - kj
    mask = (rel >= 0) & (rel < W)
    sink = sink_ref[...] * LOG2E
    lane_head = lax.broadcasted_iota(jnp.int32, (Ts, GW), 1) // HEAD
    c, s1, s2 = c_ref[...], s1_ref[...], s2_ref[...]
    tiles_per_group = GW // LANES
    for bi in range(nbat):
        rows = slice(bi * Ts, (bi + 1) * Ts)
        k2, v2 = [], []
        for t in range(KV_W // LANES):
            sl = slice(t * LANES, (t + 1) * LANES)
            kcat = jnp.concatenate([ck_ref[bi, :, sl], _rope(kn_ref[rows, sl], c, s1, s2)], axis=0)
            vcat = jnp.concatenate([cv_ref[bi, :, sl], vn_ref[rows, sl]], axis=0)
            ko_ref[bi, :, sl] = kcat[Ts:, :]
            vo_ref[bi, :, sl] = vcat[Ts:, :]
            k2.extend(x.astype(BF16) for x in _dup_heads(kcat))
            v2.extend(x.astype(BF16) for x in _dup_heads(vcat))
        s_parts = []
        for g in range(N_KV_HEADS):
            qg = jnp.concatenate(
                [_rope(q_ref[rows, (g * tiles_per_group + i) * LANES:(g * tiles_per_group + i + 1) * LANES]
                       .astype(F32), c, s1, s2)
                 for i in range(tiles_per_group)], axis=1)
            qg = qg * QK_SCALE
            qs = jnp.concatenate([jnp.where(lane_head == h, qg, 0.0) for h in range(GQ)], axis=0).astype(BF16)
            k8 = jnp.concatenate([k2[g]] * tiles_per_group, axis=1)
            s_parts.append(lax.dot_general(qs, k8, (((1,), (1,)), ((), ())), preferred_element_type=F32))
        s = jnp.where(mask, jnp.concatenate(s_parts, axis=0), -1e30)
        p = _softmax_with_sink(s, sink).astype(BF16)
        for g in range(N_KV_HEADS):
            v8 = jnp.concatenate([v2[g]] * tiles_per_group, axis=1)
            o = jnp.dot(p[g * GQ * Ts:(g + 1) * GQ * Ts], v8, preferred_element_type=F32)
            og = o[0:Ts]
            for h in range(1, GQ):
                og = jnp.where(lane_head == h, o[h * Ts:(h + 1) * Ts], og)
            gl = slice(g * GW, (g + 1) * GW)
            o_ref[rows, gl] = (og * _silu(g_ref[rows, gl].astype(F32))).astype(o_ref.dtype)


def _attn_sample(f, zskv, sinks, cache_k, cache_v, B, Ts):
    W = WINDOW
    nbat = _tile(B, (4, 2, 1))
    tabs = _rope_tables(PAST_LEN + jnp.arange(Ts))
    kcol = SHIFT_W // KV_W
    rows = nbat * Ts
    sink_col = jnp.repeat(sinks, Ts).reshape(N_Q_HEADS * Ts, 1)
    cspec = pl.BlockSpec((nbat, W, KV_W), lambda i: (i, 0, 0))
    tspec = pl.BlockSpec((Ts, LANES), lambda i: (0, 0))
    cache = jax.ShapeDtypeStruct((B, W, KV_W), F32)
    return pl.pallas_call(
        _attn_sample_kernel,
        out_shape=(jax.ShapeDtypeStruct((B * Ts, ATT_W), BF16), cache, cache),
        grid=(B // nbat,),
        in_specs=[pl.BlockSpec((rows, ATT_W), lambda i: (i, F_Q // ATT_W)),
                  pl.BlockSpec((rows, KV_W), lambda i: (i, kcol)),
                  pl.BlockSpec((rows, KV_W), lambda i: (i, kcol + 1)),
                  cspec, cspec,
                  pl.BlockSpec((rows, F_GA_BLOCK), lambda i: (i, F_GA // F_GA_BLOCK)),
                  pl.BlockSpec((N_Q_HEADS * Ts, 1), lambda i: (0, 0)),
                  tspec, tspec, tspec],
        out_specs=(pl.BlockSpec((rows, ATT_W), lambda i: (i, 0)), cspec, cspec),
        compiler_params=_params("parallel"),
        name="attn_sample",
    )(f, zskv, zskv, cache_k, cache_v, f, sink_col, *tabs)


def _mix_kernel(yr_ref, ya_ref, wr_ref, wa_ref, gr_ref, ga_ref, o_ref):
    br = jnp.dot(yr_ref[...], wr_ref[...], preferred_element_type=F32)
    ba = jnp.dot(ya_ref[...], wa_ref[...], preferred_element_type=F32)
    o_ref[...] = (_sigmoid(gr_ref[...].astype(F32)) * br
                  + _sigmoid(ga_ref[...].astype(F32)) * ba).astype(o_ref.dtype)


def _mix(y_r, y_a, w_r, w_a, f):
    M = y_r.shape[0]
    D = w_r.shape[1]
    tm = _tile(M, (1024, 512, 256, 128))
    tn = _tile(D, (512, 256, 128))
    nj = D // tn
    j0 = F_M // tn
    return pl.pallas_call(
        _mix_kernel,
        out_shape=jax.ShapeDtypeStruct((M, D), BF16),
        grid=(M // tm, nj),
        in_specs=[pl.BlockSpec((tm, RWKV_W), lambda i, j: (i, 0)),
                  pl.BlockSpec((tm, ATT_W), lambda i, j: (i, 0)),
                  pl.BlockSpec((RWKV_W, tn), lambda i, j: (0, j)),
                  pl.BlockSpec((ATT_W, tn), lambda i, j: (0, j)),
                  pl.BlockSpec((tm, tn), lambda i, j: (i, j0 + j)),
                  pl.BlockSpec((tm, tn), lambda i, j: (i, j0 + nj + j))],
        out_specs=pl.BlockSpec((tm, tn), lambda i, j: (i, j)),
        compiler_params=_params("parallel", "arbitrary"),
        name="branch_mix",
    )(y_r, y_a, w_r, w_a, f, f)


W_SLOTS = 3


def _out_kernel(mix_ref, w_hbm, x_ref, g_ref, b_ref, o_ref, wbuf, sem, *, tn, nj, nsteps):
    j = pl.program_id(1)
    s = pl.program_id(0) * nj + j

    def w_copy(step):
        col = pl.multiple_of((step % nj) * tn, tn)
        slot = step % W_SLOTS
        return pltpu.make_async_copy(w_hbm.at[:, pl.ds(col, tn)], wbuf.at[slot], sem.at[slot])

    @pl.when(s == 0)
    def _():
        for first in range(min(W_SLOTS - 1, nsteps)):
            w_copy(first).start()

    @pl.when(s + (W_SLOTS - 1) < nsteps)
    def _():
        w_copy(s + (W_SLOTS - 1)).start()

    w_copy(s).wait()
    col = pl.multiple_of(j * tn, tn)
    out = jnp.dot(mix_ref[...], wbuf[s % W_SLOTS], preferred_element_type=F32)
    o_ref[:, pl.ds(col, tn)] = ALPHA * x_ref[...] + out

    @pl.when(j == pl.num_programs(1) - 1)
    def _():
        h = o_ref[...]
        m = jnp.mean(h, axis=1, keepdims=True)
        d = h - m
        var = jnp.mean(d * d, axis=1, keepdims=True)
        o_ref[...] = d * lax.rsqrt(var + LN_EPS) * g_ref[...] + b_ref[...]


def _out_ln(mix, w_out, x, ln_g, ln_b):
    M, D = x.shape
    tm = _tile(M, (512, 256, 128))
    tn = _tile(D, (512, 256, 128))
    nj = D // tn
    row = pl.BlockSpec((1, D), lambda i, j: (0, 0))
    return pl.pallas_call(
        functools.partial(_out_kernel, tn=tn, nj=nj, nsteps=(M // tm) * nj),
        out_shape=jax.ShapeDtypeStruct((M, D), F32),
        grid=(M // tm, nj),
        in_specs=[pl.BlockSpec((tm, D), lambda i, j: (i, 0)),
                  pl.BlockSpec(memory_space=pl.ANY),
                  pl.BlockSpec((tm, tn), lambda i, j: (i, j)),
                  row, row],
        out_specs=pl.BlockSpec((tm, D), lambda i, j: (i, 0)),
        scratch_shapes=[pltpu.VMEM((W_SLOTS, D, tn), BF16), pltpu.SemaphoreType.DMA((W_SLOTS,))],
        compiler_params=_params("arbitrary", "arbitrary"),
        name="out_ln",
    )(mix, w_out, x, ln_g, ln_b)


def _k_major(x, axis):
    shp = x.shape
    n = shp[axis] // RWKV_W
    y = x.reshape(shp[:axis] + (n, RWKV_HEADS, HEAD) + shp[axis + 1:])
    y = jnp.swapaxes(y, axis + 1, axis + 2)
    return y.reshape(shp)


def _head_major(x, axis):
    shp = x.shape
    n = shp[axis] // RWKV_W
    y = x.reshape(shp[:axis] + (n, HEAD, RWKV_HEADS) + shp[axis + 1:])
    y = jnp.swapaxes(y, axis + 1, axis + 2)
    return y.reshape(shp)


def _shift_k_major(x):
    return jnp.concatenate([_k_major(x[..., :3 * RWKV_W], x.ndim - 1), x[..., 3 * RWKV_W:]], axis=-1)


def _shift_head_major(x):
    return jnp.concatenate([_head_major(x[..., :3 * RWKV_W], x.ndim - 1), x[..., 3 * RWKV_W:]], axis=-1)


def _layer(x, S0, shift0, caches, w, p):
    B, T, D = x.shape
    M = B * T
    G = B // QUAD
    x2 = x.reshape(M, D)
    xb = x2.astype(BF16)
    zskv = _matmul(xb, w["zskv"], F32, name="proj_shift_kv")
    zskv3 = zskv.reshape(B, T, ZSKV_W)
    prep = _rwkv_prep(zskv3, _shift_k_major(shift0).reshape(B, 1, SHIFT_W), p)
    ops = [a.reshape(G, T, HEAD, LANES) for a in prep[:6]]
    s0 = S0.transpose(3, 2, 0, 1).reshape(HEAD, HEAD, B * RWKV_HEADS)
    f = _matmul_cols(xb, w["all"], W_Q, BF16, name="proj_q_gates")
    g_r = _matmul(xb, w["g_r"], BF16, name="proj_gate_r")
    yn, s_fin = _wkv_scan(ops, s0)
    y_r = _rwkv_gate(yn, prep[6], g_r.reshape(B, T, RWKV_W), p["gn_w"], p["gn_b"]).reshape(M, RWKV_W)
    S_new = s_fin.reshape(HEAD, HEAD, B, RWKV_HEADS).transpose(2, 3, 1, 0)
    shift_new = _shift_head_major(zskv3[:, T - 1, :SHIFT_W])

    if caches is None:
        y_a, k_rot = _attn_prompt(f, zskv, p["sinks"], B, T)
        k_buf = k_rot.reshape(B, T, N_KV_HEADS, HEAD)[:, T - WINDOW:]
        v_buf = zskv3[:, T - WINDOW:, SHIFT_W + KV_W:].reshape(B, WINDOW, N_KV_HEADS, HEAD)
    else:
        y_a, k_buf, v_buf = _attn_sample(f, zskv, p["sinks"], caches[0], caches[1], B, T)
        k_buf = k_buf.reshape(B, WINDOW, N_KV_HEADS, HEAD)
        v_buf = v_buf.reshape(B, WINDOW, N_KV_HEADS, HEAD)

    mix = _mix(y_r, y_a, w["branch_r"], w["branch_a"], f)
    y = _out_ln(mix, w["out"], x2, p["ln_g"], p["ln_b"]).reshape(B, T, D)
    return y, S_new, shift_new, k_buf, v_buf


def kernel(x_prompt, x_sample, state_wkv, state_shift, cache_win_k, cache_win_v, w_in, mu_shift, w0, w_decay_up, a0, w_aaa_up, k_k, k_a, r_k, gn_w, gn_b, sinks, w_branch_rwkv, w_branch_attn, w_out, ln_g, ln_b):
    assert w_in.shape[0] == DEPTH
    Bp = x_prompt.shape[0]
    Bs = x_sample.shape[0]
    assert Bp % QUAD == 0 and Bs % QUAD == 0
    o_gr = SHIFT_W
    o_k = W_Q + ATT_W
    o_ga = o_k + 2 * KV_W
    wi = w_in[0].astype(BF16)
    w = {"all": wi,
         "zskv": jnp.concatenate([_shift_k_major(wi[:, :o_gr]), wi[:, o_k:o_ga]], axis=1),
         "g_r": _k_major(wi[:, o_gr:W_Q], 1),
         "branch_r": _k_major(w_branch_rwkv[0], 0).astype(BF16),
         "branch_a": w_branch_attn[0].astype(BF16), "out": w_out[0].astype(BF16)}
    lane = jnp.arange(LANES)
    km = lambda a: _k_major(a.reshape(1, RWKV_W), 1)
    p = {"mu": _shift_k_major(mu_shift), "w0": km(w0), "wup": _k_major(w_decay_up[0], 1).astype(BF16),
         "a0": km(a0), "aup": _k_major(w_aaa_up[0], 1).astype(BF16), "k_k": km(k_k), "k_a": km(k_a),
         "r_k": km(r_k), "gn_w": km(gn_w), "gn_b": km(gn_b), "sinks": sinks[0], "ln_g": ln_g, "ln_b": ln_b,
         "jmat": (lane[:, None] % RWKV_HEADS == lane[None, :] % RWKV_HEADS).astype(BF16)}

    S0p = jnp.zeros((Bp, RWKV_HEADS, HEAD, HEAD), F32)
    sh0p = jnp.zeros((Bp, SHIFT_W), F32)
    yp, a1, a2, a3, a4 = _layer(x_prompt, S0p, sh0p, None, w, p)
    caches = (cache_win_k[0].reshape(Bs, WINDOW, KV_W), cache_win_v[0].reshape(Bs, WINDOW, KV_W))
    ys, b1, b2, b3, b4 = _layer(x_sample, state_wkv[0], state_shift[0], caches, w, p)
    return (yp, ys, a1[None], a2[None], a3[None], a4[None], b1[None], b2[None], b3[None], b4[None])
```
